```python
import jax, jax.numpy as jnp
from jax import lax
import numpy as np

D_MODEL = 1024
BATCH = 4
SEQ = 8192
DEPTH = 4
DEC_BATCH = 8
DEC_SEQ = 4096
PAST_LEN = 128

N_MIXERS = 2
HEAD_DIM = 64
E_MIX = D_MODEL
A_GROUPS = ((128, 1), (512, 4), (2048, 16))
N_GROUPS_A = len(A_GROUPS)
H_A = E_MIX // HEAD_DIM
BLK_A = 64
QKV_A = 3 * N_GROUPS_A * E_MIX
IN_A = QKV_A + E_MIX
H_B = E_MIX // HEAD_DIM
KV_B = 4
REP_B = H_B // KV_B
WIN_B = 128
BLK_B = 128
DQ_B = H_B * HEAD_DIM
DKV_B = KV_B * HEAD_DIM
IN_B = DQ_B + 2 * DKV_B + E_MIX
N_LAYERS_A = (DEPTH + 1) // 2
N_LAYERS_B = DEPTH // 2
DEEPNORM_ALPHA = (2.0 * DEPTH) ** 0.25
DEEPNORM_BETA = (8.0 * DEPTH) ** -0.25
LN_EPS = 1e-5

kernel_name = "hybrid_dilated_swa_gqa_encoder"


def _alibi_slopes(n):
    return jnp.asarray(2.0 ** (-8.0 * np.arange(1, n + 1) / n), dtype=jnp.float32)


def _layernorm(h, g, b):
    h32 = h.astype(jnp.float32)
    mu = jnp.mean(h32, axis=-1, keepdims=True)
    var = jnp.mean(jnp.square(h32 - mu), axis=-1, keepdims=True)
    return ((h32 - mu) * lax.rsqrt(var + LN_EPS) * g.astype(jnp.float32) + b.astype(jnp.float32)).astype(h.dtype)


def _dilated_group(q, k, v, dil, n_side, slopes):
    B, S, H, Dh = q.shape
    L = S // dil
    nb = -(-L // BLK_A)
    Lp = nb * BLK_A

    def to_res(t):
        t = t.reshape(B, L, dil, H, Dh).transpose(0, 2, 1, 3, 4)
        return jnp.pad(t, ((0, 0), (0, 0), (0, Lp - L), (0, 0), (0, 0)))

    def band(t):
        t = jnp.pad(t, ((0, 0), (0, 0), (BLK_A, BLK_A), (0, 0), (0, 0))).reshape(B, dil, nb + 2, BLK_A, H, Dh)
        return jnp.concatenate([t[:, :, :-2], t[:, :, 1:-1], t[:, :, 2:]], axis=3)

    qb = to_res(q).reshape(B, dil, nb, BLK_A, H, Dh)
    kb = band(to_res(k))
    vb = band(to_res(v))
    s = jnp.einsum('bgnqhd,bgnkhd->bgnhqk', qb, kb, preferred_element_type=jnp.float32) * (Dh ** -0.5)
    i = jnp.arange(BLK_A)[:, None]
    j = jnp.arange(3 * BLK_A)[None, :]
    rel = j - BLK_A - i
    kpos = jnp.arange(nb)[:, None, None] * BLK_A + (j - BLK_A)[None]
    valid = (jnp.abs(rel) <= n_side)[None] & (kpos >= 0) & (kpos < L)
    bias = -slopes[:, None, None] * (dil * jnp.abs(rel)).astype(jnp.float32)[None]
    s = jnp.where(valid[:, None], s + bias, -jnp.inf)
    m = jnp.max(s, axis=-1, keepdims=True)
    p = jnp.exp(s - m)
    den = jnp.sum(p, axis=-1, keepdims=True)
    o = jnp.einsum('bgnhqk,bgnkhd->bgnqhd', p / den, vb.astype(jnp.float32))
    lse = (m + jnp.log(den))[..., 0]
    o = o.reshape(B, dil, Lp, H, Dh)[:, :, :L].transpose(0, 2, 1, 3, 4).reshape(B, S, H, Dh)
    lse = lse.transpose(0, 1, 2, 4, 3).reshape(B, dil, Lp, H)[:, :, :L].transpose(0, 2, 1, 3).reshape(B, S, H)
    return o, lse


def _mixer_a(u, w_in, w_out):
    B, S, _ = u.shape
    proj = u @ w_in
    qkv = proj[..., :QKV_A].reshape(B, S, N_GROUPS_A, 3, H_A, HEAD_DIM)
    z = proj[..., QKV_A:]
    slopes = _alibi_slopes(H_A)
    outs, lses = [], []
    for g, (win, dil) in enumerate(A_GROUPS):
        o, l = _dilated_group(qkv[:, :, g, 0], qkv[:, :, g, 1], qkv[:, :, g, 2], dil, win // (2 * dil), slopes)
        outs.append(o)
        lses.append(l)
    wts = jax.nn.softmax(jnp.stack(lses), axis=0)[..., None]
    att = jnp.sum(wts * jnp.stack(outs), axis=0).reshape(B, S, E_MIX).astype(u.dtype)
    return (att * jax.nn.silu(z)) @ w_out


def _mixer_b(u, w_in, w_out, sink):
    B, S, _ = u.shape
    nb = S // BLK_B
    proj = u @ w_in
    q = proj[..., :DQ_B].reshape(B, nb, BLK_B, KV_B, REP_B, HEAD_DIM)
    k = proj[..., DQ_B:DQ_B + DKV_B].reshape(B, S, KV_B, HEAD_DIM)
    v = proj[..., DQ_B + DKV_B:DQ_B + 2 * DKV_B].reshape(B, S, KV_B, HEAD_DIM)
    z = proj[..., DQ_B + 2 * DKV_B:]

    def band(t):
        t = jnp.pad(t, ((0, 0), (BLK_B, BLK_B), (0, 0), (0, 0))).reshape(B, nb + 2, BLK_B, KV_B, HEAD_DIM)
        return jnp.concatenate([t[:, :-2], t[:, 1:-1], t[:, 2:]], axis=2)

    kb, vb = band(k), band(v)
    s = jnp.einsum('bnqgrd,bnkgd->bngrqk', q, kb, preferred_element_type=jnp.float32) * (HEAD_DIM ** -0.5)
    i = jnp.arange(BLK_B)[:, None]
    j = jnp.arange(3 * BLK_B)[None, :]
    rel = j - BLK_B - i
    kpos = jnp.arange(nb)[:, None, None] * BLK_B + (j - BLK_B)[None]
    valid = (jnp.abs(rel) <= WIN_B)[None] & (kpos >= 0) & (kpos < S)
    slopes = _alibi_slopes(H_B).reshape(KV_B, REP_B)
    bias = -slopes[:, :, None, None] * jnp.abs(rel).astype(jnp.float32)
    s = jnp.where(valid[:, None, None], s + bias, -jnp.inf)
    snk = sink.astype(jnp.float32).reshape(KV_B, REP_B)[:, :, None, None]
    m = jnp.maximum(jnp.max(s, axis=-1, keepdims=True), snk)
    p = jnp.exp(s - m)
    den = jnp.sum(p, axis=-1, keepdims=True) + jnp.exp(snk - m)
    o = jnp.einsum('bngrqk,bnkgd->bnqgrd', p / den, vb.astype(jnp.float32))
    o = o.reshape(B, S, E_MIX).astype(u.dtype)
    return (o * jax.nn.silu(z)) @ w_out


def _trunk(x, c, w_mod, b_mod, ln_g, ln_b, w_in_a, w_out_a, w_in_b, w_out_b, sink_b):
    for l in range(DEPTH):
        mod = jax.nn.silu(c) @ w_mod[l] + b_mod[l]
        shift, scale, gate = jnp.split(mod[:, None, :], 3, axis=-1)
        u = x * (1 + scale) + shift
        if l % N_MIXERS == 0:
            y = _mixer_a(u, w_in_a[l // N_MIXERS], w_out_a[l // N_MIXERS])
        else:
            y = _mixer_b(u, w_in_b[l // N_MIXERS], w_out_b[l // N_MIXERS], sink_b[l // N_MIXERS])
        x = _layernorm(DEEPNORM_ALPHA * x + gate * y, ln_g[l], ln_b[l])
    return x


def setup_inputs(seed: int = 0) -> dict:
    key = jax.random.key(seed)
    ks = jax.random.split(key, 15)
    f32 = jnp.float32
    d_sc = D_MODEL ** -0.5
    e_sc = E_MIX ** -0.5
    return {
        "x_prompt": jax.random.normal(ks[0], (BATCH, SEQ, D_MODEL), f32),
        "x_sample": jax.random.normal(ks[1], (DEC_BATCH, DEC_SEQ, D_MODEL), f32),
        "c_prompt": jax.random.normal(ks[2], (BATCH, D_MODEL), f32),
        "c_sample": jax.random.normal(ks[3], (DEC_BATCH, D_MODEL), f32),
        "w_mod": jax.random.normal(ks[4], (DEPTH, D_MODEL, 3 * D_MODEL), f32) * (0.5 * d_sc),
        "b_mod": jax.random.normal(ks[5], (DEPTH, 3 * D_MODEL), f32) * 0.01,
        "ln_g": 1.0 + 0.02 * jax.random.normal(ks[6], (DEPTH, D_MODEL), f32),
        "ln_b": 0.02 * jax.random.normal(ks[7], (DEPTH, D_MODEL), f32),
        "w_in_a": jax.random.normal(ks[8], (N_LAYERS_A, D_MODEL, IN_A), f32) * d_sc,
        "w_out_a": jax.random.normal(ks[9], (N_LAYERS_A, E_MIX, D_MODEL), f32) * (e_sc * DEEPNORM_BETA),
        "w_in_b": jax.random.normal(ks[10], (N_LAYERS_B, D_MODEL, IN_B), f32) * d_sc,
        "w_out_b": jax.random.normal(ks[11], (N_LAYERS_B, E_MIX, D_MODEL), f32) * (e_sc * DEEPNORM_BETA),
        "sink_b": jax.random.normal(ks[12], (N_LAYERS_B, H_B), f32),
    }


def reference(x_prompt, x_sample, c_prompt, c_sample, w_mod, b_mod, ln_g, ln_b,
              w_in_a, w_out_a, w_in_b, w_out_b, sink_b):
    y_prompt = _trunk(x_prompt, c_prompt, w_mod, b_mod, ln_g, ln_b, w_in_a, w_out_a, w_in_b, w_out_b, sink_b)
    y_sample = _trunk(x_sample, c_sample, w_mod, b_mod, ln_g, ln_b, w_in_a, w_out_a, w_in_b, w_out_b, sink_b)
    return (y_prompt, y_sample)
```

```python
import functools
import math

import numpy as np
import jax
import jax.numpy as jnp
from jax import lax
from jax.experimental import pallas as pl
from jax.experimental.pallas import tpu as pltpu

F32 = jnp.float32
BF16 = jnp.bfloat16

D_MODEL = 1024
DEPTH = 4
HEAD_DIM = 64
N_HEADS = 16
E_MIX = 1024
A_GROUPS = ((128, 1), (512, 4), (2048, 16))
QKV_A = 3 * len(A_GROUPS) * E_MIX
IN_A = QKV_A + E_MIX
KV_B = 4
REP_B = N_HEADS // KV_B
WIN_B = 128
DKV_B = KV_B * HEAD_DIM
IN_B = E_MIX + 2 * DKV_B + E_MIX
DEEPNORM_ALPHA = (2.0 * DEPTH) ** 0.25
LN_EPS = 1e-5

LANES = 128
N_PAIRS = E_MIX // LANES
MQ = 128
LOG2E = math.log2(math.e)
NEG = -1e30
VMEM_LIMIT = 56 * 1024 * 1024


def _alibi_slopes(n):
    return np.asarray(2.0 ** (-8.0 * np.arange(1, n + 1) / n), dtype=np.float32)


def _mod_kernel(c_ref, w_ref, b_ref, o_ref):
    c = c_ref[...]
    s = (c * jax.nn.sigmoid(c)).astype(BF16)
    acc = jnp.dot(s, w_ref[...].astype(BF16), preferred_element_type=F32)
    o_ref[...] = acc + b_ref[...]


def _modulation(c_all, w_mod, b_mod):
    nb = c_all.shape[0]
    tn = 1024
    return pl.pallas_call(
        _mod_kernel,
        grid=(DEPTH, 3 * D_MODEL // tn),
        in_specs=[
            pl.BlockSpec((nb, D_MODEL), lambda l, j: (0, 0)),
            pl.BlockSpec((None, D_MODEL, tn), lambda l, j: (l, 0, j)),
            pl.BlockSpec((None, 1, tn), lambda l, j: (l, 0, j)),
        ],
        out_specs=pl.BlockSpec((None, nb, tn), lambda l, j: (l, 0, j)),
        out_shape=jax.ShapeDtypeStruct((DEPTH, nb, 3 * D_MODEL), F32),
        compiler_params=pltpu.CompilerParams(
            dimension_semantics=("parallel", "parallel"), vmem_limit_bytes=VMEM_LIMIT),
        name="modulation",
    )(c_all, w_mod, b_mod.reshape(DEPTH, 1, 3 * D_MODEL))


def _inproj_kernel(x_ref, mod_ref, w_ref, cs_ref, o_ref, u_ref):
    @pl.when(pl.program_id(2) == 0)
    def _():
        shift = mod_ref[:, 0:D_MODEL]
        scale = mod_ref[:, D_MODEL:2 * D_MODEL]
        u_ref[...] = (x_ref[...] * (1.0 + scale) + shift).astype(BF16)

    acc = jnp.dot(u_ref[...], w_ref[...], preferred_element_type=F32)
    o_ref[...] = (acc * cs_ref[...]).astype(BF16)


def _inproj(x, mod, w, colscale, tn):
    b, s, _ = x.shape
    n = w.shape[1]
    tm = 1024
    return pl.pallas_call(
        _inproj_kernel,
        grid=(b, s // tm, n // tn),
        in_specs=[
            pl.BlockSpec((None, tm, D_MODEL), lambda bi, i, j: (bi, i, 0)),
            pl.BlockSpec((None, 1, 3 * D_MODEL), lambda bi, i, j: (bi, 0, 0)),
            pl.BlockSpec((D_MODEL, tn), lambda bi, i, j: (0, j)),
            pl.BlockSpec((1, tn), lambda bi, i, j: (0, j)),
        ],
        out_specs=pl.BlockSpec((None, tm, tn), lambda bi, i, j: (bi, i, j)),
        out_shape=jax.ShapeDtypeStruct((b, s, n), BF16),
        scratch_shapes=[pltpu.VMEM((tm, D_MODEL), BF16)],
        compiler_params=pltpu.CompilerParams(
            dimension_semantics=("parallel", "parallel", "arbitrary"),
            vmem_limit_bytes=VMEM_LIMIT),
        name="inproj",
    )(x, mod, w, colscale)


def _attn_kernel(*refs, tq, halo, nk, kv_shift, has_sink, emit_lse, n_ub):
    q_ref, ko_ref, kp_ref, kn_ref, vo_ref, vp_ref, vn_ref, tab_ref = refs[:8]
    pos = 8
    sink_ref = None
    if has_sink:
        sink_ref = refs[pos]
        pos += 1
    o_ref = refs[pos]
    pos += 1
    lse_ref = None
    if emit_lse:
        lse_ref = refs[pos]
        pos += 1
    kbuf, vbuf = refs[pos], refs[pos + 1]

    ub = pl.program_id(2)
    n_sub = tq // MQ

    kbuf[0:halo] = kp_ref[...]
    kbuf[halo:halo + tq] = ko_ref[...]
    kbuf[halo + tq:halo + tq + halo] = kn_ref[...]
    vbuf[0:halo] = vp_ref[...]
    vbuf[halo:halo + tq] = vo_ref[...]
    vbuf[halo + tq:halo + tq + halo] = vn_ref[...]

    lane = lax.broadcasted_iota(jnp.int32, (MQ, LANES), 1)
    lo = lane < HEAD_DIM
    mask_lo = jnp.where(lo, 1.0, 0.0).astype(BF16)
    mask_hi = jnp.where(lo, 0.0, 1.0).astype(BF16)

    def sub_body(sb, carry):
        r0 = pl.multiple_of(sb * MQ, MQ)
        is_first = jnp.logical_and(ub == 0, sb == 0)
        is_last = jnp.logical_and(ub == n_ub - 1, sb == n_sub - 1)
        var = jnp.where(is_first, 0, jnp.where(is_last, 2, 1))

        def pair_body(pb, lse_blk):
            c0 = pl.multiple_of(pb * LANES, LANES)
            kc0 = pl.multiple_of(lax.shift_right_logical(pb, kv_shift) * LANES, LANES)
            qp = q_ref[pl.ds(r0, MQ), pl.ds(c0, LANES)]
            qs = jnp.concatenate([qp * mask_lo, qp * mask_hi], axis=0)
            kpair = kbuf[pl.ds(r0, nk), pl.ds(kc0, LANES)]
            vpair = vbuf[pl.ds(r0, nk), pl.ds(kc0, LANES)]
            s = lax.dot_general(qs, kpair, (((1,), (1,)), ((), ())),
                                preferred_element_type=F32)
            s = s + tab_ref[var, pb]
            m = jnp.max(s, axis=-1, keepdims=True)
            if has_sink:
                snk = sink_ref[pb][:, 0:1]
                m = jnp.maximum(m, snk)
            p = jnp.exp2(s - m)
            den = jnp.sum(p, axis=-1, keepdims=True)
            if has_sink:
                den = den + jnp.exp2(snk - m)
            pv = jnp.dot(p.astype(BF16), vpair, preferred_element_type=F32)
            pv = pv * (1.0 / den)
            o = jnp.where(lo, pv[:MQ], pv[MQ:])
            o_ref[pl.ds(r0, MQ), pl.ds(c0, LANES)] = o.astype(BF16)
            if emit_lse:
                l2 = m + jnp.log2(den)
                lse_blk = jnp.where(lane == 2 * pb, l2[:MQ], lse_blk)
                lse_blk = jnp.where(lane == 2 * pb + 1, l2[MQ:], lse_blk)
            return lse_blk

        lse_blk = lax.fori_loop(0, N_PAIRS, pair_body, jnp.zeros((MQ, LANES), F32))
        if emit_lse:
            lse_ref[pl.ds(r0, MQ), :] = lse_blk
        return carry

    lax.fori_loop(0, n_sub, sub_body, 0)


def _attention(proj, *, dil, halo, q_col, k_col, v_col, kv_width, kv_shift,
               table, sink, emit_lse):
    b, s, c = proj.shape
    seq = s // dil
    tq = min(512, seq)
    n_ub = seq // tq
    nk = MQ + 2 * halo
    assert seq % tq == 0 and seq >= 2 * MQ and tq % halo == 0
    assert q_col % E_MIX == 0 and k_col % kv_width == 0 and v_col % kv_width == 0
    assert dil == 1 or (c % E_MIX == 0 and c % kv_width == 0)
    view = proj.reshape(b, seq, dil * c)
    qblk, kblk, vblk = q_col // E_MIX, k_col // kv_width, v_col // kv_width
    hb = tq // halo
    last_hb = seq // halo - 1

    def own(width, cblk):
        per_tok = c // width
        return pl.BlockSpec((None, tq, width),
                            lambda bi, r, u: (bi, u, r * per_tok + cblk))

    def prev(width, cblk):
        per_tok = c // width
        return pl.BlockSpec((None, halo, width),
                            lambda bi, r, u: (bi, jnp.maximum(u * hb - 1, 0), r * per_tok + cblk))

    def nxt(width, cblk):
        per_tok = c // width
        return pl.BlockSpec((None, halo, width),
                            lambda bi, r, u: (bi, jnp.minimum((u + 1) * hb, last_hb), r * per_tok + cblk))

    in_specs = [
        own(E_MIX, qblk),
        own(kv_width, kblk), prev(kv_width, kblk), nxt(kv_width, kblk),
        own(kv_width, vblk), prev(kv_width, vblk), nxt(kv_width, vblk),
        pl.BlockSpec(table.shape, lambda bi, r, u: (0, 0, 0, 0)),
    ]
    args = [view] * 7 + [table]
    if sink is not None:
        in_specs.append(pl.BlockSpec(sink.shape, lambda bi, r, u: (0, 0, 0)))
        args.append(sink)

    out_shape = [jax.ShapeDtypeStruct((b, seq, dil * E_MIX), BF16)]
    out_specs = [pl.BlockSpec((None, tq, E_MIX), lambda bi, r, u: (bi, u, r))]
    if emit_lse:
        out_shape.append(jax.ShapeDtypeStruct((b, seq, dil * LANES), F32))
        out_specs.append(pl.BlockSpec((None, tq, LANES), lambda bi, r, u: (bi, u, r)))

    kern = functools.partial(
        _attn_kernel, tq=tq, halo=halo, nk=nk, kv_shift=kv_shift,
        has_sink=sink is not None, emit_lse=emit_lse, n_ub=n_ub)
    outs = pl.pallas_call(
        kern,
        grid=(b, dil, n_ub),
        in_specs=in_specs,
        out_specs=out_specs,
        out_shape=out_shape,
        scratch_shapes=[pltpu.VMEM((tq + 2 * halo, kv_width), BF16),
                        pltpu.VMEM((tq + 2 * halo, kv_width), BF16)],
        compiler_params=pltpu.CompilerParams(
            dimension_semantics=("parallel", "parallel", "parallel"),
            vmem_limit_bytes=VMEM_LIMIT),
        name="attention",
    )(*args)
    o = outs[0].reshape(b, s, E_MIX)
    if emit_lse:
        return o, outs[1].reshape(b, s, LANES)
    return o, None


def _out_kernel(*refs, n_groups):
    o_refs = refs[:n_groups]
    pos = n_groups
    if n_groups > 1:
        l_refs = refs[pos:pos + n_groups]
        e_ref = refs[pos + n_groups]
        pos += n_groups + 1
    z_ref, x_ref, mod_ref, w_ref, g_ref, b_ref, out_ref = refs[pos:pos + 7]

    if n_groups == 1:
        att = o_refs[0][...].astype(F32)
    else:
        ls = [l[...] for l in l_refs]
        mx = ls[0]
        for l in ls[1:]:
            mx = jnp.maximum(mx, l)
        es = [jnp.exp2(l - mx) for l in ls]
        tot = es[0]
        for e in es[1:]:
            tot = tot + e
        inv = 1.0 / tot
        att = None
        for e, o_ref in zip(es, o_refs):
            wexp = jnp.dot((e * inv).astype(BF16), e_ref[...], preferred_element_type=F32)
            term = wexp * o_ref[...].astype(F32)
            att = term if att is None else att + term

    z = z_ref[...].astype(F32)
    h = att * (z * jax.nn.sigmoid(z))
    y = jnp.dot(h.astype(BF16), w_ref[...], preferred_element_type=F32)
    gate = mod_ref[:, 2 * D_MODEL:3 * D_MODEL]
    v = DEEPNORM_ALPHA * x_ref[...] + gate * y
    mu = jnp.mean(v, axis=-1, keepdims=True)
    vc = v - mu
    var = jnp.mean(vc * vc, axis=-1, keepdims=True)
    out_ref[...] = vc * lax.rsqrt(var + LN_EPS) * g_ref[...] + b_ref[...]


def _outproj(os_, ls_, expand, proj, z_col, x, mod, w, g, bta):
    b, s, c = proj.shape
    tm = 512
    n_groups = len(os_)
    assert z_col % E_MIX == 0
    zblk = z_col // E_MIX
    row = lambda bi, i: (bi, i, 0)
    in_specs = [pl.BlockSpec((None, tm, E_MIX), row) for _ in os_]
    args = list(os_)
    if n_groups > 1:
        in_specs += [pl.BlockSpec((None, tm, LANES), row) for _ in ls_]
        in_specs.append(pl.BlockSpec(expand.shape, lambda bi, i: (0, 0)))
        args += list(ls_) + [expand]
    in_specs += [
        pl.BlockSpec((None, tm, E_MIX), lambda bi, i: (bi, i, zblk)),
        pl.BlockSpec((None, tm, D_MODEL), row),
        pl.BlockSpec((None, 1, 3 * D_MODEL), lambda bi, i: (bi, 0, 0)),
        pl.BlockSpec((E_MIX, D_MODEL), lambda bi, i: (0, 0)),
        pl.BlockSpec((1, D_MODEL), lambda bi, i: (0, 0)),
        pl.BlockSpec((1, D_MODEL), lambda bi, i: (0, 0)),
    ]
    args += [proj, x, mod, w, g, bta]
    return pl.pallas_call(
        functools.partial(_out_kernel, n_groups=n_groups),
        grid=(b, s // tm),
        in_specs=in_specs,
        out_specs=pl.BlockSpec((None, tm, D_MODEL), row),
        out_shape=jax.ShapeDtypeStruct((b, s, D_MODEL), F32),
        compiler_params=pltpu.CompilerParams(
            dimension_semantics=("parallel", "parallel"), vmem_limit_bytes=VMEM_LIMIT),
        name="outproj",
    )(*args)


def _bias_table(head_of, dil, halo):
    nk = MQ + 2 * halo
    slopes = _alibi_slopes(N_HEADS)
    i = np.arange(MQ)[:, None]
    j = np.arange(nk)[None, :]
    rel = j - halo - i
    band = np.abs(rel) <= halo
    valid = np.stack([band & (j >= halo), band, band & (j < MQ + halo)])
    dist = (dil * np.abs(rel)).astype(np.float32)
    heads = np.asarray([[head_of(pb, 0), head_of(pb, 1)] for pb in range(N_PAIRS)])
    bias = -(slopes[heads][:, :, None, None] * dist[None, None]) * np.float32(LOG2E)
    tab = np.where(valid[:, None, None], bias[None], np.float32(NEG))
    return jnp.asarray(tab.reshape(3, N_PAIRS, 2 * MQ, nk), dtype=F32)


def _head_a(pb, half):
    return 2 * pb + half


def _head_b(pb, half):
    return (2 * (pb // REP_B) + half) * REP_B + pb % REP_B


def _perm_b():
    n = np.arange(E_MIX)
    pb, half, d = n // LANES, (n % LANES) // HEAD_DIM, n % HEAD_DIM
    g = 2 * (pb // REP_B) + half
    r = pb % REP_B
    return g * (REP_B * HEAD_DIM) + r * HEAD_DIM + d


def _colscale(n, q_cols):
    cs = np.ones((1, n), np.float32)
    for lo_, hi_ in q_cols:
        cs[:, lo_:hi_] = HEAD_DIM ** -0.5 * LOG2E
    return jnp.asarray(cs)


def _expand_matrix():
    e = np.zeros((LANES, E_MIX), np.float32)
    for h in range(N_HEADS):
        e[h, h * HEAD_DIM:(h + 1) * HEAD_DIM] = 1.0
    return jnp.asarray(e, dtype=BF16)


def _layer_a(x, mod, w_in, w_out, cs, tables, expand, g, bta):
    proj = _inproj(x, mod, w_in, cs, tn=1024)
    os_, ls_ = [], []
    for gi, (win, dil) in enumerate(A_GROUPS):
        base = gi * 3 * E_MIX
        o, l = _attention(proj, dil=dil, halo=win // (2 * dil), q_col=base,
                          k_col=base + E_MIX, v_col=base + 2 * E_MIX, kv_width=E_MIX,
                          kv_shift=0, table=tables[gi], sink=None, emit_lse=True)
        os_.append(o)
        ls_.append(l)
    return _outproj(os_, ls_, expand, proj, QKV_A, x, mod, w_out, g, bta)


def _layer_b(x, mod, w_in, w_out, cs, table, sink, g, bta):
    proj = _inproj(x, mod, w_in, cs, tn=IN_B // 2)
    o, _ = _attention(proj, dil=1, halo=WIN_B, q_col=0, k_col=2 * E_MIX, v_col=2 * E_MIX + DKV_B,
                      kv_width=DKV_B, kv_shift=2, table=table, sink=sink, emit_lse=False)
    return _outproj([o], None, None, proj, E_MIX, x, mod, w_out, g, bta)


def kernel(x_prompt, x_sample, c_prompt, c_sample, w_mod, b_mod, ln_g, ln_b,
           w_in_a, w_out_a, w_in_b, w_out_b, sink_b):
    nbp = c_prompt.shape[0]
    mods = _modulation(jnp.concatenate([c_prompt, c_sample], axis=0), w_mod, b_mod)

    perm = _perm_b()
    col_b = np.concatenate([perm, E_MIX + 2 * DKV_B + perm, E_MIX + np.arange(2 * DKV_B)])
    w_in_a16 = w_in_a.astype(BF16)
    w_out_a16 = w_out_a.astype(BF16)
    w_in_b16 = w_in_b[:, :, col_b].astype(BF16)
    w_out_b16 = w_out_b[:, perm, :].astype(BF16)

    cs_a = _colscale(IN_A, [(gi * 3 * E_MIX, gi * 3 * E_MIX + E_MIX) for gi in range(len(A_GROUPS))])
    cs_b = _colscale(IN_B, [(0, E_MIX)])
    tables_a = [_bias_table(_head_a, dil, win // (2 * dil)) for win, dil in A_GROUPS]
    table_b = _bias_table(_head_b, 1, WIN_B)
    expand = _expand_matrix()
    heads_b = np.asarray([[_head_b(pb, 0), _head_b(pb, 1)] for pb in range(N_PAIRS)])

    def trunk(x, mod_all):
        for l in range(DEPTH):
            mod = mod_all[l][:, None, :]
            g = ln_g[l][None, :]
            bta = ln_b[l][None, :]
            if l % 2 == 0:
                x = _layer_a(x, mod, w_in_a16[l // 2], w_out_a16[l // 2], cs_a, tables_a,
                             expand, g, bta)
            else:
                snk = sink_b[l // 2][heads_b] * LOG2E
                snk = jnp.broadcast_to(snk[:, :, None, None], (N_PAIRS, 2, MQ, LANES))
                snk = snk.reshape(N_PAIRS, 2 * MQ, LANES).astype(F32)
                x = _layer_b(x, mod, w_in_b16[l // 2], w_out_b16[l // 2], cs_b, table_b,
                             snk, g, bta)
        return x

    y_prompt = trunk(x_prompt, mods[:, :nbp])
    y_sample = trunk(x_sample, mods[:, nbp:])
    return (y_prompt, y_sample)
```

```python
import functools
import math

import numpy as np
import jax
import jax.numpy as jnp
from jax import lax
from jax.experimental import pallas as pl
from jax.experimental.pallas import tpu as pltpu

F32 = jnp.float32
BF16 = jnp.bfloat16

D_MODEL = 1024
DEPTH = 4
HEAD_DIM = 64
N_HEADS = 16
E_MIX = 1024
A_GROUPS = ((128, 1), (512, 4), (2048, 16))
QKV_A = 3 * len(A_GROUPS) * E_MIX
IN_A = QKV_A + E_MIX
KV_B = 4
REP_B = N_HEADS // KV_B
WIN_B = 128
DKV_B = KV_B * HEAD_DIM
IN_B = E_MIX + 2 * DKV_B + E_MIX
DEEPNORM_ALPHA = (2.0 * DEPTH) ** 0.25
LN_EPS = 1e-5

LANES = 128
N_PAIRS = E_MIX // LANES
MQ = 128
LOG2E = math.log2(math.e)
NEG = -1e30
VMEM_LIMIT = 56 * 1024 * 1024


def _alibi_slopes(n):
    return np.asarray(2.0 ** (-8.0 * np.arange(1, n + 1) / n), dtype=np.float32)


def _mod_kernel(c_ref, w_ref, b_ref, o_ref):
    c = c_ref[...]
    s = (c * jax.nn.sigmoid(c)).astype(BF16)
    acc = jnp.dot(s, w_ref[...].astype(BF16), preferred_element_type=F32)
    o_ref[...] = acc + b_ref[...]


def _modulation(c_all, w_mod, b_mod):
    nb = c_all.shape[0]
    tn = 1024
    return pl.pallas_call(
        _mod_kernel,
        grid=(DEPTH, 3 * D_MODEL // tn),
        in_specs=[
            pl.BlockSpec((nb, D_MODEL), lambda l, j: (0, 0)),
            pl.BlockSpec((None, D_MODEL, tn), lambda l, j: (l, 0, j)),
            pl.BlockSpec((None, 1, tn), lambda l, j: (l, 0, j)),
        ],
        out_specs=pl.BlockSpec((None, nb, tn), lambda l, j: (l, 0, j)),
        out_shape=jax.ShapeDtypeStruct((DEPTH, nb, 3 * D_MODEL), F32),
        compiler_params=pltpu.CompilerParams(
            dimension_semantics=("parallel", "parallel"), vmem_limit_bytes=VMEM_LIMIT),
        name="modulation",
    )(c_all, w_mod, b_mod.reshape(DEPTH, 1, 3 * D_MODEL))


N_CHUNK = 1024


def _inproj_kernel(x_ref, mod_ref, w_ref, cs_ref, o_ref, *scratch, dil):
    tm = x_ref.shape[0]
    n = tm // dil
    ncol = w_ref.shape[1]
    shift = mod_ref[:, 0:D_MODEL]
    scale = mod_ref[:, D_MODEL:2 * D_MODEL]
    um = x_ref[...] * (1.0 + scale) + shift
    if dil == 1:
        u = um.astype(BF16)
    else:
        u_ref, xs_refs = scratch[0], scratch[1:]
        for c, xs_ref in enumerate(xs_refs):
            xs_ref[...] = um[:, c * LANES:(c + 1) * LANES]
        for r in range(dil):
            for c, xs_ref in enumerate(xs_refs):
                u_ref[r * n:(r + 1) * n, c * LANES:(c + 1) * LANES] = (
                    xs_ref[pl.ds(r, n, stride=dil), :].astype(BF16))
        u = u_ref[...]
    for c0 in range(0, ncol, N_CHUNK):
        c1 = min(c0 + N_CHUNK, ncol)
        acc = jnp.dot(u, w_ref[:, c0:c1], preferred_element_type=F32) * cs_ref[:, c0:c1]
        acc = acc.astype(BF16)
        for r in range(dil):
            o_ref[:, r * ncol + c0:r * ncol + c1] = acc[r * n:(r + 1) * n, :]


def _inproj(x, mod, w, colscale, dil):
    b, s, _ = x.shape
    ncol = w.shape[1]
    tm = 512
    n = tm // dil
    assert s % tm == 0 and n % 16 == 0
    scratch = []
    if dil > 1:
        scratch = [pltpu.VMEM((tm, D_MODEL), BF16)]
        scratch += [pltpu.VMEM((tm, LANES), F32) for _ in range(D_MODEL // LANES)]
    return pl.pallas_call(
        functools.partial(_inproj_kernel, dil=dil),
        grid=(b, s // tm),
        in_specs=[
            pl.BlockSpec((None, tm, D_MODEL), lambda bi, i: (bi, i, 0)),
            pl.BlockSpec((None, 1, 3 * D_MODEL), lambda bi, i: (bi, 0, 0)),
            pl.BlockSpec((D_MODEL, ncol), lambda bi, i: (0, 0)),
            pl.BlockSpec((1, ncol), lambda bi, i: (0, 0)),
        ],
        out_specs=pl.BlockSpec((None, n, dil * ncol), lambda bi, i: (bi, i, 0)),
        out_shape=jax.ShapeDtypeStruct((b, s // dil, dil * ncol), BF16),
        scratch_shapes=scratch,
        compiler_params=pltpu.CompilerParams(
            dimension_semantics=("parallel", "parallel"), vmem_limit_bytes=VMEM_LIMIT),
        name="inproj",
    )(x, mod, w, colscale)


def _attn_kernel(*refs, tq, halo, nk, kv_shift, has_sink, emit_lse, n_ub):
    q_ref, ko_ref, kp_ref, kn_ref, vo_ref, vp_ref, vn_ref, tab_ref = refs[:8]
    pos = 8
    sink_ref = None
    if has_sink:
        sink_ref = refs[pos]
        pos += 1
    o_ref = refs[pos]
    pos += 1
    lse_ref = None
    if emit_lse:
        lse_ref = refs[pos]
        pos += 1
    kbuf, vbuf = refs[pos], refs[pos + 1]

    ub = pl.program_id(2)
    n_sub = tq // MQ

    kbuf[0:halo] = kp_ref[...]
    kbuf[halo:halo + tq] = ko_ref[...]
    kbuf[halo + tq:halo + tq + halo] = kn_ref[...]
    vbuf[0:halo] = vp_ref[...]
    vbuf[halo:halo + tq] = vo_ref[...]
    vbuf[halo + tq:halo + tq + halo] = vn_ref[...]

    lane = lax.broadcasted_iota(jnp.int32, (MQ, LANES), 1)
    lo = lane < HEAD_DIM
    mask_lo = jnp.where(lo, 1.0, 0.0).astype(BF16)
    mask_hi = jnp.where(lo, 0.0, 1.0).astype(BF16)

    def sub_body(sb, carry):
        r0 = pl.multiple_of(sb * MQ, MQ)
        is_first = jnp.logical_and(ub == 0, sb == 0)
        is_last = jnp.logical_and(ub == n_ub - 1, sb == n_sub - 1)
        var = jnp.where(is_first, 0, jnp.where(is_last, 2, 1))

        def pair_body(pb, lse_blk):
            c0 = pl.multiple_of(pb * LANES, LANES)
            kc0 = pl.multiple_of(lax.shift_right_logical(pb, kv_shift) * LANES, LANES)
            qp = q_ref[pl.ds(r0, MQ), pl.ds(c0, LANES)]
            qs = jnp.concatenate([qp * mask_lo, qp * mask_hi], axis=0)
            kpair = kbuf[pl.ds(r0, nk), pl.ds(kc0, LANES)]
            vpair = vbuf[pl.ds(r0, nk), pl.ds(kc0, LANES)]
            s = lax.dot_general(qs, kpair, (((1,), (1,)), ((), ())),
                                preferred_element_type=F32)
            s = s + tab_ref[var, pb]
            m = jnp.max(s, axis=-1, keepdims=True)
            if has_sink:
                snk = sink_ref[pb][:, 0:1]
                m = jnp.maximum(m, snk)
            p = jnp.exp2(s - m)
            den = jnp.sum(p, axis=-1, keepdims=True)
            if has_sink:
                den = den + jnp.exp2(snk - m)
            pv = jnp.dot(p.astype(BF16), vpair, preferred_element_type=F32)
            pv = pv * (1.0 / den)
            o = jnp.where(lo, pv[:MQ], pv[MQ:])
            o_ref[pl.ds(r0, MQ), pl.ds(c0, LANES)] = o.astype(BF16)
            if emit_lse:
                l2 = m + jnp.log2(den)
                lse_blk = jnp.where(lane == 2 * pb, l2[:MQ], lse_blk)
                lse_blk = jnp.where(lane == 2 * pb + 1, l2[MQ:], lse_blk)
            return lse_blk

        lse_blk = lax.fori_loop(0, N_PAIRS, pair_body, jnp.zeros((MQ, LANES), F32))
        if emit_lse:
            lse_ref[pl.ds(r0, MQ), :] = lse_blk
        return carry

    lax.fori_loop(0, n_sub, sub_body, 0)


def _attention(view, *, dil, halo, q_col, k_col, v_col, kv_width, kv_shift,
               table, sink, emit_lse):
    b, seq, dc = view.shape
    c = dc // dil
    tq = min(512, seq)
    n_ub = seq // tq
    nk = MQ + 2 * halo
    assert seq % tq == 0 and seq >= 2 * MQ and tq % halo == 0
    assert q_col % E_MIX == 0 and k_col % kv_width == 0 and v_col % kv_width == 0
    assert dil == 1 or (c % E_MIX == 0 and c % kv_width == 0)
    qblk, kblk, vblk = q_col // E_MIX, k_col // kv_width, v_col // kv_width
    hb = tq // halo
    last_hb = seq // halo - 1

    def own(width, cblk):
        per_tok = c // width
        return pl.BlockSpec((None, tq, width),
                            lambda bi, r, u: (bi, u, r * per_tok + cblk))

    def prev(width, cblk):
        per_tok = c // width
        return pl.BlockSpec((None, halo, width),
                            lambda bi, r, u: (bi, jnp.maximum(u * hb - 1, 0), r * per_tok + cblk))

    def nxt(width, cblk):
        per_tok = c // width
        return pl.BlockSpec((None, halo, width),
                            lambda bi, r, u: (bi, jnp.minimum((u + 1) * hb, last_hb), r * per_tok + cblk))

    in_specs = [
        own(E_MIX, qblk),
        own(kv_width, kblk), prev(kv_width, kblk), nxt(kv_width, kblk),
        own(kv_width, vblk), prev(kv_width, vblk), nxt(kv_width, vblk),
        pl.BlockSpec(table.shape, lambda bi, r, u: (0, 0, 0, 0)),
    ]
    args = [view] * 7 + [table]
    if sink is not None:
        in_specs.append(pl.BlockSpec(sink.shape, lambda bi, r, u: (0, 0, 0)))
        args.append(sink)

    out_shape = [jax.ShapeDtypeStruct((b, seq, dil * E_MIX), BF16)]
    out_specs = [pl.BlockSpec((None, tq, E_MIX), lambda bi, r, u: (bi, u, r))]
    if emit_lse:
        out_shape.append(jax.ShapeDtypeStruct((b, seq, dil * LANES), F32))
        out_specs.append(pl.BlockSpec((None, tq, LANES), lambda bi, r, u: (bi, u, r)))

    kern = functools.partial(
        _attn_kernel, tq=tq, halo=halo, nk=nk, kv_shift=kv_shift,
        has_sink=sink is not None, emit_lse=emit_lse, n_ub=n_ub)
    outs = pl.pallas_call(
        kern,
        grid=(b, dil, n_ub),
        in_specs=in_specs,
        out_specs=out_specs,
        out_shape=out_shape,
        scratch_shapes=[pltpu.VMEM((tq + 2 * halo, kv_width), BF16),
                        pltpu.VMEM((tq + 2 * halo, kv_width), BF16)],
        compiler_params=pltpu.CompilerParams(
            dimension_semantics=("parallel", "parallel", "parallel"),
            vmem_limit_bytes=VMEM_LIMIT),
        name="attention",
    )(*args)
    return (outs[0], outs[1]) if emit_lse else (outs[0], None)


PERM_ROWS = 256


def _out_kernel(*refs, dils):
    n_groups = len(dils)
    o_refs = refs[:n_groups]
    pos = n_groups
    if n_groups > 1:
        l_refs = refs[pos:pos + n_groups]
        pos += n_groups
        p_refs = {}
        for d in dils:
            if d > 1:
                p_refs[d] = refs[pos]
                pos += 1
        e_ref = refs[pos]
        pos += 1
    z_ref, x_ref, mod_ref, w_ref, g_ref, b_ref, out_ref = refs[pos:pos + 7]
    l_scratch = refs[pos + 7:]
    tm = x_ref.shape[0]

    def natural_rows(o_ref, d):
        nb = PERM_ROWS // d
        outs = []
        for t in range(tm // PERM_ROWS):
            stacked = jnp.concatenate(
                [o_ref[t * nb:(t + 1) * nb, r * E_MIX:(r + 1) * E_MIX] for r in range(d)], axis=0)
            outs.append(jnp.dot(p_refs[d][...], stacked, preferred_element_type=F32))
        return jnp.concatenate(outs, axis=0)

    if n_groups == 1:
        att = o_refs[0][...].astype(F32)
    else:
        ls = []
        si = 0
        for l_ref, d in zip(l_refs, dils):
            if d == 1:
                ls.append(l_ref[...])
            else:
                scr = l_scratch[si]
                si += 1
                n = tm // d
                for r in range(d):
                    scr[pl.ds(r, n, stride=d), :] = l_ref[:, r * LANES:(r + 1) * LANES]
                ls.append(scr[...])
        mx = ls[0]
        for l in ls[1:]:
            mx = jnp.maximum(mx, l)
        es = [jnp.exp2(l - mx) for l in ls]
        tot = es[0]
        for e in es[1:]:
            tot = tot + e
        inv = 1.0 / tot
        att = None
        for e, o_ref, d in zip(es, o_refs, dils):
            wexp = jnp.dot((e * inv).astype(BF16), e_ref[...], preferred_element_type=F32)
            o = o_ref[...].astype(F32) if d == 1 else natural_rows(o_ref, d)
            term = wexp * o
            att = term if att is None else att + term

    z = z_ref[...].astype(F32)
    h = att * (z * jax.nn.sigmoid(z))
    y = jnp.dot(h.astype(BF16), w_ref[...], preferred_element_type=F32)
    gate = mod_ref[:, 2 * D_MODEL:3 * D_MODEL]
    v = DEEPNORM_ALPHA * x_ref[...] + gate * y
    mu = jnp.mean(v, axis=-1, keepdims=True)
    vc = v - mu
    var = jnp.mean(vc * vc, axis=-1, keepdims=True)
    out_ref[...] = vc * lax.rsqrt(var + LN_EPS) * g_ref[...] + b_ref[...]


def _perm_matrix(d):
    nb = PERM_ROWS // d
    p = np.zeros((PERM_ROWS, PERM_ROWS), np.float32)
    for r in range(d):
        for u in range(nb):
            p[u * d + r, r * nb + u] = 1.0
    return jnp.asarray(p, dtype=BF16)


def _outproj(os_, ls_, dils, expand, zsrc, z_col, x, mod, w, g, bta):
    b, s, _ = x.shape
    tm = 512
    n_groups = len(os_)
    assert z_col % E_MIX == 0 and s % tm == 0 and tm % PERM_ROWS == 0
    zblk = z_col // E_MIX
    row = lambda bi, i: (bi, i, 0)
    in_specs = [pl.BlockSpec((None, tm // d, d * E_MIX), row) for d in dils]
    args = list(os_)
    scratch = []
    if n_groups > 1:
        in_specs += [pl.BlockSpec((None, tm // d, d * LANES), row) for d in dils]
        args += list(ls_)
        for d in dils:
            if d > 1:
                in_specs.append(pl.BlockSpec((PERM_ROWS, PERM_ROWS), lambda bi, i: (0, 0)))
                args.append(_perm_matrix(d))
                scratch.append(pltpu.VMEM((tm, LANES), F32))
        in_specs.append(pl.BlockSpec(expand.shape, lambda bi, i: (0, 0)))
        args.append(expand)
    in_specs += [
        pl.BlockSpec((None, tm, E_MIX), lambda bi, i: (bi, i, zblk)),
        pl.BlockSpec((None, tm, D_MODEL), row),
        pl.BlockSpec((None, 1, 3 * D_MODEL), lambda bi, i: (bi, 0, 0)),
        pl.BlockSpec((E_MIX, D_MODEL), lambda bi, i: (0, 0)),
        pl.BlockSpec((1, D_MODEL), lambda bi, i: (0, 0)),
        pl.BlockSpec((1, D_MODEL), lambda bi, i: (0, 0)),
    ]
    args += [zsrc, x, mod, w, g, bta]
    return pl.pallas_call(
        functools.partial(_out_kernel, dils=tuple(dils)),
        grid=(b, s // tm),
        in_specs=in_specs,
        out_specs=pl.BlockSpec((None, tm, D_MODEL), row),
        out_shape=jax.ShapeDtypeStruct((b, s, D_MODEL), F32),
        scratch_shapes=scratch,
        compiler_params=pltpu.CompilerParams(
            dimension_semantics=("parallel", "parallel"), vmem_limit_bytes=VMEM_LIMIT),
        name="outproj",
    )(*args)


def _bias_table(head_of, dil, halo):
    nk = MQ + 2 * halo
    slopes = _alibi_slopes(N_HEADS)
    i = np.arange(MQ)[:, None]
    j = np.arange(nk)[None, :]
    rel = j - halo - i
    band = np.abs(rel) <= halo
    valid = np.stack([band & (j >= halo), band, band & (j < MQ + halo)])
    dist = (dil * np.abs(rel)).astype(np.float32)
    heads = np.asarray([[head_of(pb, 0), head_of(pb, 1)] for pb in range(N_PAIRS)])
    bias = -(slopes[heads][:, :, None, None] * dist[None, None]) * np.float32(LOG2E)
    tab = np.where(valid[:, None, None], bias[None], np.float32(NEG))
    return jnp.asarray(tab.reshape(3, N_PAIRS, 2 * MQ, nk), dtype=F32)


def _head_a(pb, half):
    return 2 * pb + half


def _head_b(pb, half):
    return (2 * (pb // REP_B) + half) * REP_B + pb % REP_B


def _perm_b():
    n = np.arange(E_MIX)
    pb, half, d = n // LANES, (n % LANES) // HEAD_DIM, n % HEAD_DIM
    g = 2 * (pb // REP_B) + half
    r = pb % REP_B
    return g * (REP_B * HEAD_DIM) + r * HEAD_DIM + d


def _colscale(n, q_cols):
    cs = np.ones((1, n), np.float32)
    for lo_, hi_ in q_cols:
        cs[:, lo_:hi_] = HEAD_DIM ** -0.5 * LOG2E
    return jnp.asarray(cs)


def _expand_matrix():
    e = np.zeros((LANES, E_MIX), np.float32)
    for h in range(N_HEADS):
        e[h, h * HEAD_DIM:(h + 1) * HEAD_DIM] = 1.0
    return jnp.asarray(e, dtype=BF16)


def _layer_a(x, mod, w_groups, w_out, cs_groups, tables, expand, g, bta):
    os_, ls_, dils = [], [], []
    zsrc = None
    for gi, (win, dil) in enumerate(A_GROUPS):
        view = _inproj(x, mod, w_groups[gi], cs_groups[gi], dil)
        if gi == 0:
            zsrc = view
        o, l = _attention(view, dil=dil, halo=win // (2 * dil), q_col=0, k_col=E_MIX,
                          v_col=2 * E_MIX, kv_width=E_MIX, kv_shift=0, table=tables[gi],
                          sink=None, emit_lse=True)
        os_.append(o)
        ls_.append(l)
        dils.append(dil)
    return _outproj(os_, ls_, dils, expand, zsrc, 3 * E_MIX, x, mod, w_out, g, bta)


def _layer_b(x, mod, w_in, w_out, cs, table, sink, g, bta):
    proj = _inproj(x, mod, w_in, cs, 1)
    o, _ = _attention(proj, dil=1, halo=WIN_B, q_col=0, k_col=2 * E_MIX, v_col=2 * E_MIX + DKV_B,
                      kv_width=DKV_B, kv_shift=2, table=table, sink=sink, emit_lse=False)
    return _outproj([o], None, [1], None, proj, E_MIX, x, mod, w_out, g, bta)


def kernel(x_prompt, x_sample, c_prompt, c_sample, w_mod, b_mod, ln_g, ln_b,
           w_in_a, w_out_a, w_in_b, w_out_b, sink_b):
    nbp = c_prompt.shape[0]
    mods = _modulation(jnp.concatenate([c_prompt, c_sample], axis=0), w_mod, b_mod)

    assert A_GROUPS[0][1] == 1
    grp = 3 * E_MIX
    w_a = w_in_a.astype(BF16)
    w_in_a16 = [[jnp.concatenate([w_a[l][:, :grp], w_a[l][:, QKV_A:]], axis=1)]
                + [w_a[l][:, gi * grp:(gi + 1) * grp] for gi in range(1, len(A_GROUPS))]
                for l in range(w_in_a.shape[0])]
    w_out_a16 = w_out_a.astype(BF16)

    def regroup_b(w, axis):
        shp = w.shape[:axis] + (KV_B // 2, 2, REP_B, HEAD_DIM) + w.shape[axis + 1:]
        w = jnp.swapaxes(w.reshape(shp), axis + 1, axis + 2)
        return w.reshape(w.shape[:axis] + (E_MIX,) + w.shape[axis + 4:])

    w_b = w_in_b.astype(BF16)
    w_in_b16 = jnp.concatenate([regroup_b(w_b[:, :, :E_MIX], 2),
                                regroup_b(w_b[:, :, E_MIX + 2 * DKV_B:], 2),
                                w_b[:, :, E_MIX:E_MIX + 2 * DKV_B]], axis=2)
    w_out_b16 = regroup_b(w_out_b.astype(BF16), 1)

    cs_a = [_colscale(w.shape[1], [(0, E_MIX)]) for w in w_in_a16[0]]
    cs_b = _colscale(IN_B, [(0, E_MIX)])
    tables_a = [_bias_table(_head_a, dil, win // (2 * dil)) for win, dil in A_GROUPS]
    table_b = _bias_table(_head_b, 1, WIN_B)
    expand = _expand_matrix()
    heads_b = np.asarray([[_head_b(pb, 0), _head_b(pb, 1)] for pb in range(N_PAIRS)])

    def trunk(x, mod_all):
        for l in range(DEPTH):
            mod = mod_all[l][:, None, :]
            g = ln_g[l][None, :]
            bta = ln_b[l][None, :]
            if l % 2 == 0:
                x = _layer_a(x, mod, w_in_a16[l // 2], w_out_a16[l // 2], cs_a, tables_a,
                             expand, g, bta)
            else:
                snk = sink_b[l // 2][heads_b] * LOG2E
                snk = jnp.broadcast_to(snk[:, :, None, None], (N_PAIRS, 2, MQ, LANES))
                snk = snk.reshape(N_PAIRS, 2 * MQ, LANES).astype(F32)
                x = _layer_b(x, mod, w_in_b16[l // 2], w_out_b16[l // 2], cs_b, table_b,
                             snk, g, bta)
        return x

    y_prompt = trunk(x_prompt, mods[:, :nbp])
    y_sample = trunk(x_sample, mods[:, nbp:])
    return (y_prompt, y_sample)
```

```python
import functools
import math

import numpy as np
import jax
import jax.numpy as jnp
from jax import lax
from jax.experimental import pallas as pl
from jax.experimental.pallas import tpu as pltpu

F32 = jnp.float32
BF16 = jnp.bfloat16

D_MODEL = 1024
DEPTH = 4
HEAD_DIM = 64
N_HEADS = 16
E_MIX = 1024
A_GROUPS = ((128, 1), (512, 4), (2048, 16))
QKV_A = 3 * len(A_GROUPS) * E_MIX
IN_A = QKV_A + E_MIX
KV_B = 4
REP_B = N_HEADS // KV_B
WIN_B = 128
DKV_B = KV_B * HEAD_DIM
IN_B = E_MIX + 2 * DKV_B + E_MIX
DEEPNORM_ALPHA = (2.0 * DEPTH) ** 0.25
LN_EPS = 1e-5

LANES = 128
N_PAIRS = E_MIX // LANES
MQ = 128
SCORE_LOOKAHEAD = 2
LOG2E = math.log2(math.e)
NEG = -1e30
VMEM_LIMIT = 56 * 1024 * 1024


def _alibi_slopes(n):
    return np.asarray(2.0 ** (-8.0 * np.arange(1, n + 1) / n), dtype=np.float32)


def _mod_kernel(c_ref, w_ref, b_ref, o_ref):
    c = c_ref[...]
    s = (c * jax.nn.sigmoid(c)).astype(BF16)
    acc = jnp.dot(s, w_ref[...].astype(BF16), preferred_element_type=F32)
    o_ref[...] = acc + b_ref[...]


def _modulation(c_all, w_mod, b_mod):
    nb = c_all.shape[0]
    tn = 1024
    return pl.pallas_call(
        _mod_kernel,
        grid=(DEPTH, 3 * D_MODEL // tn),
        in_specs=[
            pl.BlockSpec((nb, D_MODEL), lambda l, j: (0, 0)),
            pl.BlockSpec((None, D_MODEL, tn), lambda l, j: (l, 0, j)),
            pl.BlockSpec((None, 1, tn), lambda l, j: (l, 0, j)),
        ],
        out_specs=pl.BlockSpec((None, nb, tn), lambda l, j: (l, 0, j)),
        out_shape=jax.ShapeDtypeStruct((DEPTH, nb, 3 * D_MODEL), F32),
        compiler_params=pltpu.CompilerParams(
            dimension_semantics=("parallel", "parallel"), vmem_limit_bytes=VMEM_LIMIT),
        name="modulation",
    )(c_all, w_mod, b_mod.reshape(DEPTH, 1, 3 * D_MODEL))


N_CHUNK = 1024


def _inproj_kernel(x_ref, mod_ref, w_ref, cs_ref, o_ref, *scratch, dil):
    tm = x_ref.shape[0]
    n = tm // dil
    ncol = w_ref.shape[1]
    shift = mod_ref[:, 0:D_MODEL]
    scale = mod_ref[:, D_MODEL:2 * D_MODEL]
    um = x_ref[...] * (1.0 + scale) + shift
    if dil == 1:
        u = um.astype(BF16)
    else:
        u_ref, xs_refs = scratch[0], scratch[1:]
        for c, xs_ref in enumerate(xs_refs):
            xs_ref[...] = um[:, c * LANES:(c + 1) * LANES]
        for r in range(dil):
            for c, xs_ref in enumerate(xs_refs):
                u_ref[r * n:(r + 1) * n, c * LANES:(c + 1) * LANES] = (
                    xs_ref[pl.ds(r, n, stride=dil), :].astype(BF16))
        u = u_ref[...]
    for c0 in range(0, ncol, N_CHUNK):
        c1 = min(c0 + N_CHUNK, ncol)
        acc = jnp.dot(u, w_ref[:, c0:c1], preferred_element_type=F32) * cs_ref[:, c0:c1]
        acc = acc.astype(BF16)
        for r in range(dil):
            o_ref[:, r * ncol + c0:r * ncol + c1] = acc[r * n:(r + 1) * n, :]


def _inproj(x, mod, w, colscale, dil):
    b, s, _ = x.shape
    ncol = w.shape[1]
    tm = 512
    n = tm // dil
    assert s % tm == 0 and n % 16 == 0
    scratch = []
    if dil > 1:
        scratch = [pltpu.VMEM((tm, D_MODEL), BF16)]
        scratch += [pltpu.VMEM((tm, LANES), F32) for _ in range(D_MODEL // LANES)]
    return pl.pallas_call(
        functools.partial(_inproj_kernel, dil=dil),
        grid=(b, s // tm),
        in_specs=[
            pl.BlockSpec((None, tm, D_MODEL), lambda bi, i: (bi, i, 0)),
            pl.BlockSpec((None, 1, 3 * D_MODEL), lambda bi, i: (bi, 0, 0)),
            pl.BlockSpec((D_MODEL, ncol), lambda bi, i: (0, 0)),
            pl.BlockSpec((1, ncol), lambda bi, i: (0, 0)),
        ],
        out_specs=pl.BlockSpec((None, n, dil * ncol), lambda bi, i: (bi, i, 0)),
        out_shape=jax.ShapeDtypeStruct((b, s // dil, dil * ncol), BF16),
        scratch_shapes=scratch,
        compiler_params=pltpu.CompilerParams(
            dimension_semantics=("parallel", "parallel"), vmem_limit_bytes=VMEM_LIMIT),
        name="inproj",
    )(x, mod, w, colscale)


def _attn_kernel(*refs, tq, halo, nk, kv_rep, has_sink, n_ub, sub_unroll):
    q_ref, ko_ref, kp_ref, kn_ref, vo_ref, vp_ref, vn_ref, tab_ref = refs[:8]
    pos = 8
    sink_ref = None
    if has_sink:
        sink_ref = refs[pos]
        pos += 1
    o_ref, m_ref, den_ref, kbuf, vbuf = refs[pos:pos + 5]

    ub = pl.program_id(2)
    n_sub = tq // MQ

    kbuf[0:halo] = kp_ref[...]
    kbuf[halo:halo + tq] = ko_ref[...]
    kbuf[halo + tq:halo + tq + halo] = kn_ref[...]
    vbuf[0:halo] = vp_ref[...]
    vbuf[halo:halo + tq] = vo_ref[...]
    vbuf[halo + tq:halo + tq + halo] = vn_ref[...]

    lane = lax.broadcasted_iota(jnp.int32, (MQ, LANES), 1)
    lo = lane < HEAD_DIM
    mask_lo = jnp.where(lo, 1.0, 0.0).astype(BF16)
    mask_hi = jnp.where(lo, 0.0, 1.0).astype(BF16)

    def block_rows(sb):
        r0 = sb * MQ if isinstance(sb, int) else pl.multiple_of(sb * MQ, MQ)
        return r0, pl.ds(r0, MQ)

    def score_stage(sb, pb):
        r0, rows = block_rows(sb)
        is_first = jnp.logical_and(ub == 0, sb == 0)
        is_last = jnp.logical_and(ub == n_ub - 1, sb == n_sub - 1)
        var = jnp.where(is_first, 0, jnp.where(is_last, 2, 1))
        kcols = slice((pb // kv_rep) * LANES, (pb // kv_rep + 1) * LANES)
        qp = q_ref[rows, pb * LANES:(pb + 1) * LANES]
        qs = jnp.concatenate([qp * mask_lo, qp * mask_hi], axis=0)
        kpair = kbuf[pl.ds(r0, nk), kcols]
        s = lax.dot_general(qs, kpair, (((1,), (1,)), ((), ())),
                            preferred_element_type=F32)
        return s + tab_ref[var, pb]

    def value_stage(sb, pb, s):
        r0, rows = block_rows(sb)
        kcols = slice((pb // kv_rep) * LANES, (pb // kv_rep + 1) * LANES)
        vpair = vbuf[pl.ds(r0, nk), kcols]
        m = jnp.max(s, axis=-1, keepdims=True)
        if has_sink:
            snk = [sink_ref[2 * pb + half] for half in range(2)]
            m = jnp.broadcast_to(m, (2 * MQ, LANES))
            m = jnp.concatenate([jnp.maximum(m[:MQ], snk[0]), jnp.maximum(m[MQ:], snk[1])],
                                axis=0)
            p = jnp.concatenate([jnp.exp2(s[:, t * LANES:(t + 1) * LANES] - m)
                                 for t in range(nk // LANES)], axis=1)
        else:
            p = jnp.exp2(s - m)
        den = jnp.sum(p, axis=-1, keepdims=True)
        if has_sink:
            den = jnp.broadcast_to(den, (2 * MQ, LANES)) + jnp.concatenate(
                [jnp.exp2(snk[0] - m[:MQ]), jnp.exp2(snk[1] - m[MQ:])], axis=0)
        pv = jnp.dot(p.astype(BF16), vpair, preferred_element_type=F32)
        o_ref[rows, pb * LANES:(pb + 1) * LANES] = jnp.where(lo, pv[:MQ], pv[MQ:]).astype(BF16)
        for half in range(2):
            hl = slice(2 * pb + half, 2 * pb + half + 1)
            hl_src = hl if has_sink else slice(0, 1)
            m_ref[rows, hl] = m[half * MQ:(half + 1) * MQ, hl_src]
            den_ref[rows, hl] = den[half * MQ:(half + 1) * MQ, hl_src]

    def run_blocks(sbs):
        for sb in sbs:
            _, rows = block_rows(sb)
            m_ref[rows, :] = jnp.zeros((MQ, LANES), F32)
            den_ref[rows, :] = jnp.ones((MQ, LANES), F32)
        items = [(sb, pb) for sb in sbs for pb in range(N_PAIRS)]
        pending = [score_stage(*it) for it in items[:SCORE_LOOKAHEAD]]
        for i, it in enumerate(items):
            if i + SCORE_LOOKAHEAD < len(items):
                pending.append(score_stage(*items[i + SCORE_LOOKAHEAD]))
            value_stage(*it, pending.pop(0))

    if sub_unroll >= n_sub:
        run_blocks(list(range(n_sub)))
    else:
        def body(it, carry):
            run_blocks([it * sub_unroll + j for j in range(sub_unroll)])
            return carry
        lax.fori_loop(0, n_sub // sub_unroll, body, 0)


def _attention(view, *, dil, halo, q_col, k_col, v_col, kv_width, kv_rep, table, sink,
               sub_unroll):
    b, seq, dc = view.shape
    c = dc // dil
    tq = min(512, seq)
    n_ub = seq // tq
    nk = MQ + 2 * halo
    assert seq % tq == 0 and seq >= 2 * MQ and tq % halo == 0
    assert q_col % E_MIX == 0 and k_col % kv_width == 0 and v_col % kv_width == 0
    assert dil == 1 or (c % E_MIX == 0 and c % kv_width == 0)
    qblk, kblk, vblk = q_col // E_MIX, k_col // kv_width, v_col // kv_width
    hb = tq // halo
    last_hb = seq // halo - 1

    def own(width, cblk):
        per_tok = c // width
        return pl.BlockSpec((None, tq, width),
                            lambda bi, r, u: (bi, u, r * per_tok + cblk))

    def prev(width, cblk):
        per_tok = c // width
        return pl.BlockSpec((None, halo, width),
                            lambda bi, r, u: (bi, jnp.maximum(u * hb - 1, 0), r * per_tok + cblk))

    def nxt(width, cblk):
        per_tok = c // width
        return pl.BlockSpec((None, halo, width),
                            lambda bi, r, u: (bi, jnp.minimum((u + 1) * hb, last_hb), r * per_tok + cblk))

    in_specs = [
        own(E_MIX, qblk),
        own(kv_width, kblk), prev(kv_width, kblk), nxt(kv_width, kblk),
        own(kv_width, vblk), prev(kv_width, vblk), nxt(kv_width, vblk),
        pl.BlockSpec(table.shape, lambda bi, r, u: (0, 0, 0, 0)),
    ]
    args = [view] * 7 + [table]
    if sink is not None:
        in_specs.append(pl.BlockSpec(memory_space=pltpu.SMEM))
        args.append(sink)

    out_shape = [jax.ShapeDtypeStruct((b, seq, dil * E_MIX), BF16)]
    out_specs = [pl.BlockSpec((None, tq, E_MIX), lambda bi, r, u: (bi, u, r))]
    for _ in range(2):
        out_shape.append(jax.ShapeDtypeStruct((b, seq, dil * LANES), F32))
        out_specs.append(pl.BlockSpec((None, tq, LANES), lambda bi, r, u: (bi, u, r)))

    kern = functools.partial(
        _attn_kernel, tq=tq, halo=halo, nk=nk, kv_rep=kv_rep,
        has_sink=sink is not None, n_ub=n_ub, sub_unroll=sub_unroll)
    return pl.pallas_call(
        kern,
        grid=(b, dil, n_ub),
        in_specs=in_specs,
        out_specs=out_specs,
        out_shape=out_shape,
        scratch_shapes=[pltpu.VMEM((tq + 2 * halo, kv_width), BF16),
                        pltpu.VMEM((tq + 2 * halo, kv_width), BF16)],
        compiler_params=pltpu.CompilerParams(
            dimension_semantics=("parallel", "parallel", "parallel"),
            vmem_limit_bytes=VMEM_LIMIT),
        name="attention",
    )(*args)


PERM_ROWS = 256


def _out_kernel(*refs, dils):
    n_groups = len(dils)
    o_refs = refs[:n_groups]
    m_refs = refs[n_groups:2 * n_groups]
    d_refs = refs[2 * n_groups:3 * n_groups]
    pos = 3 * n_groups
    p_refs = {}
    for d in dils:
        if d > 1:
            p_refs[d] = refs[pos]
            pos += 1
    e_ref, z_ref, x_ref, mod_ref, w_ref, g_ref, b_ref, out_ref = refs[pos:pos + 8]
    stat_scratch = list(refs[pos + 8:])
    tm = x_ref.shape[0]

    def natural_rows(o_ref, d):
        nb = PERM_ROWS // d
        outs = []
        for t in range(tm // PERM_ROWS):
            stacked = jnp.concatenate(
                [o_ref[t * nb:(t + 1) * nb, r * E_MIX:(r + 1) * E_MIX] for r in range(d)], axis=0)
            outs.append(jnp.dot(p_refs[d][...], stacked, preferred_element_type=F32))
        return jnp.concatenate(outs, axis=0)

    def natural_stat(ref, d):
        if d == 1:
            return ref[...]
        scr = stat_scratch.pop(0)
        n = tm // d
        for r in range(d):
            scr[pl.ds(r, n, stride=d), :] = ref[:, r * LANES:(r + 1) * LANES]
        return scr[...]

    ms = [natural_stat(r, d) for r, d in zip(m_refs, dils)]
    dens = [natural_stat(r, d) for r, d in zip(d_refs, dils)]
    mx = ms[0]
    for m in ms[1:]:
        mx = jnp.maximum(mx, m)
    es = [jnp.exp2(m - mx) for m in ms]
    tot = es[0] * dens[0]
    for e, dn in zip(es[1:], dens[1:]):
        tot = tot + e * dn
    inv = 1.0 / tot
    att = None
    for e, o_ref, d in zip(es, o_refs, dils):
        wexp = jnp.dot((e * inv).astype(BF16), e_ref[...], preferred_element_type=F32)
        o = o_ref[...].astype(F32) if d == 1 else natural_rows(o_ref, d)
        term = wexp * o
        att = term if att is None else att + term

    z = z_ref[...].astype(F32)
    h = att * (z * jax.nn.sigmoid(z))
    y = jnp.dot(h.astype(BF16), w_ref[...], preferred_element_type=F32)
    gate = mod_ref[:, 2 * D_MODEL:3 * D_MODEL]
    v = DEEPNORM_ALPHA * x_ref[...] + gate * y
    mu = jnp.mean(v, axis=-1, keepdims=True)
    vc = v - mu
    var = jnp.mean(vc * vc, axis=-1, keepdims=True)
    out_ref[...] = vc * lax.rsqrt(var + LN_EPS) * g_ref[...] + b_ref[...]


def _perm_matrix(d):
    nb = PERM_ROWS // d
    p = np.zeros((PERM_ROWS, PERM_ROWS), np.float32)
    for r in range(d):
        for u in range(nb):
            p[u * d + r, r * nb + u] = 1.0
    return jnp.asarray(p, dtype=BF16)


def _outproj(attn_outs, dils, expand, zsrc, z_col, x, mod, w, g, bta):
    b, s, _ = x.shape
    tm = 512
    assert z_col % E_MIX == 0 and s % tm == 0 and tm % PERM_ROWS == 0
    zblk = z_col // E_MIX
    row = lambda bi, i: (bi, i, 0)
    in_specs = [pl.BlockSpec((None, tm // d, d * E_MIX), row) for d in dils]
    in_specs += [pl.BlockSpec((None, tm // d, d * LANES), row) for d in dils] * 2
    args = [t[0] for t in attn_outs] + [t[1] for t in attn_outs] + [t[2] for t in attn_outs]
    scratch = []
    for d in dils:
        if d > 1:
            in_specs.append(pl.BlockSpec((PERM_ROWS, PERM_ROWS), lambda bi, i: (0, 0)))
            args.append(_perm_matrix(d))
            scratch += [pltpu.VMEM((tm, LANES), F32)] * 2
    in_specs.append(pl.BlockSpec(expand.shape, lambda bi, i: (0, 0)))
    args.append(expand)
    in_specs += [
        pl.BlockSpec((None, tm, E_MIX), lambda bi, i: (bi, i, zblk)),
        pl.BlockSpec((None, tm, D_MODEL), row),
        pl.BlockSpec((None, 1, 3 * D_MODEL), lambda bi, i: (bi, 0, 0)),
        pl.BlockSpec((E_MIX, D_MODEL), lambda bi, i: (0, 0)),
        pl.BlockSpec((1, D_MODEL), lambda bi, i: (0, 0)),
        pl.BlockSpec((1, D_MODEL), lambda bi, i: (0, 0)),
    ]
    args += [zsrc, x, mod, w, g, bta]
    return pl.pallas_call(
        functools.partial(_out_kernel, dils=tuple(dils)),
        grid=(b, s // tm),
        in_specs=in_specs,
        out_specs=pl.BlockSpec((None, tm, D_MODEL), row),
        out_shape=jax.ShapeDtypeStruct((b, s, D_MODEL), F32),
        scratch_shapes=scratch,
        compiler_params=pltpu.CompilerParams(
            dimension_semantics=("parallel", "parallel"), vmem_limit_bytes=VMEM_LIMIT),
        name="outproj",
    )(*args)


def _bias_table(head_of, dil, halo):
    nk = MQ + 2 * halo
    slopes = _alibi_slopes(N_HEADS)
    i = np.arange(MQ)[:, None]
    j = np.arange(nk)[None, :]
    rel = j - halo - i
    band = np.abs(rel) <= halo
    valid = np.stack([band & (j >= halo), band, band & (j < MQ + halo)])
    dist = (dil * np.abs(rel)).astype(np.float32)
    heads = np.asarray([[head_of(pb, 0), head_of(pb, 1)] for pb in range(N_PAIRS)])
    bias = -(slopes[heads][:, :, None, None] * dist[None, None]) * np.float32(LOG2E)
    tab = np.where(valid[:, None, None], bias[None], np.float32(NEG))
    return jnp.asarray(tab.reshape(3, N_PAIRS, 2 * MQ, nk), dtype=F32)


def _head_a(pb, half):
    return 2 * pb + half


def _head_b(pb, half):
    return (2 * (pb // REP_B) + half) * REP_B + pb % REP_B


def _perm_b():
    n = np.arange(E_MIX)
    pb, half, d = n // LANES, (n % LANES) // HEAD_DIM, n % HEAD_DIM
    g = 2 * (pb // REP_B) + half
    r = pb % REP_B
    return g * (REP_B * HEAD_DIM) + r * HEAD_DIM + d


def _colscale(n, q_cols):
    cs = np.ones((1, n), np.float32)
    for lo_, hi_ in q_cols:
        cs[:, lo_:hi_] = HEAD_DIM ** -0.5 * LOG2E
    return jnp.asarray(cs)


def _expand_matrix():
    e = np.zeros((LANES, E_MIX), np.float32)
    for h in range(N_HEADS):
        e[h, h * HEAD_DIM:(h + 1) * HEAD_DIM] = 1.0
    return jnp.asarray(e, dtype=BF16)


def _layer_a(x, mod, w_groups, w_out, cs_groups, tables, expand, g, bta):
    outs, dils = [], []
    zsrc = None
    for gi, (win, dil) in enumerate(A_GROUPS):
        view = _inproj(x, mod, w_groups[gi], cs_groups[gi], dil)
        if gi == 0:
            zsrc = view
        outs.append(_attention(view, dil=dil, halo=win // (2 * dil), q_col=0, k_col=E_MIX,
                               v_col=2 * E_MIX, kv_width=E_MIX, kv_rep=1, table=tables[gi],
                               sink=None, sub_unroll=4))
        dils.append(dil)
    return _outproj(outs, dils, expand, zsrc, 3 * E_MIX, x, mod, w_out, g, bta)


def _layer_b(x, mod, w_in, w_out, cs, table, sink, expand, g, bta):
    proj = _inproj(x, mod, w_in, cs, 1)
    out = _attention(proj, dil=1, halo=WIN_B, q_col=0, k_col=2 * E_MIX, v_col=2 * E_MIX + DKV_B,
                     kv_width=DKV_B, kv_rep=REP_B, table=table, sink=sink, sub_unroll=4)
    return _outproj([out], [1], expand, proj, E_MIX, x, mod, w_out, g, bta)


def kernel(x_prompt, x_sample, c_prompt, c_sample, w_mod, b_mod, ln_g, ln_b,
           w_in_a, w_out_a, w_in_b, w_out_b, sink_b):
    nbp = c_prompt.shape[0]
    mods = _modulation(jnp.concatenate([c_prompt, c_sample], axis=0), w_mod, b_mod)

    assert A_GROUPS[0][1] == 1
    grp = 3 * E_MIX
    w_a = w_in_a.astype(BF16)
    w_in_a16 = [[jnp.concatenate([w_a[l][:, :grp], w_a[l][:, QKV_A:]], axis=1)]
                + [w_a[l][:, gi * grp:(gi + 1) * grp] for gi in range(1, len(A_GROUPS))]
                for l in range(w_in_a.shape[0])]
    w_out_a16 = w_out_a.astype(BF16)

    def regroup_b(w, axis):
        shp = w.shape[:axis] + (KV_B // 2, 2, REP_B, HEAD_DIM) + w.shape[axis + 1:]
        w = jnp.swapaxes(w.reshape(shp), axis + 1, axis + 2)
        return w.reshape(w.shape[:axis] + (E_MIX,) + w.shape[axis + 4:])

    w_b = w_in_b.astype(BF16)
    w_in_b16 = jnp.concatenate([regroup_b(w_b[:, :, :E_MIX], 2),
                                regroup_b(w_b[:, :, E_MIX + 2 * DKV_B:], 2),
                                w_b[:, :, E_MIX:E_MIX + 2 * DKV_B]], axis=2)
    w_out_b16 = regroup_b(w_out_b.astype(BF16), 1)

    cs_a = [_colscale(w.shape[1], [(0, E_MIX)]) for w in w_in_a16[0]]
    cs_b = _colscale(IN_B, [(0, E_MIX)])
    tables_a = [_bias_table(_head_a, dil, win // (2 * dil)) for win, dil in A_GROUPS]
    table_b = _bias_table(_head_b, 1, WIN_B)
    expand = _expand_matrix()
    heads_b = np.asarray([[_head_b(pb, 0), _head_b(pb, 1)] for pb in range(N_PAIRS)])

    def trunk(x, mod_all):
        for l in range(DEPTH):
            mod = mod_all[l][:, None, :]
            g = ln_g[l][None, :]
            bta = ln_b[l][None, :]
            if l % 2 == 0:
                x = _layer_a(x, mod, w_in_a16[l // 2], w_out_a16[l // 2], cs_a, tables_a,
                             expand, g, bta)
            else:
                snk = (sink_b[l // 2][heads_b.reshape(-1)] * LOG2E).astype(F32)
                x = _layer_b(x, mod, w_in_b16[l // 2], w_out_b16[l // 2], cs_b, table_b,
                             snk, expand, g, bta)
        return x

    y_prompt = trunk(x_prompt, mods[:, :nbp])
    y_sample = trunk(x_sample, mods[:, nbp:])
    return (y_prompt, y_sample)
```

```python
import functools
import math

import numpy as np
import jax
import jax.numpy as jnp
from jax import lax
from jax.experimental import pallas as pl
from jax.experimental.pallas import tpu as pltpu

F32 = jnp.float32
BF16 = jnp.bfloat16

D_MODEL = 1024
DEPTH = 4
HEAD_DIM = 64
N_HEADS = 16
E_MIX = 1024
A_GROUPS = ((128, 1), (512, 4), (2048, 16))
QKV_A = 3 * len(A_GROUPS) * E_MIX
KV_B = 4
REP_B = N_HEADS // KV_B
WIN_B = 128
DKV_B = KV_B * HEAD_DIM
IN_B = E_MIX + 2 * DKV_B + E_MIX
DEEPNORM_ALPHA = (2.0 * DEPTH) ** 0.25
LN_EPS = 1e-5

LANES = 128
N_PAIRS = E_MIX // LANES
MQ = 128
SCORE_LOOKAHEAD = 2
LOG2E = math.log2(math.e)
NEG = -1e30
VMEM_LIMIT = 56 * 1024 * 1024


def _alibi_slopes(n):
    return np.asarray(2.0 ** (-8.0 * np.arange(1, n + 1) / n), dtype=np.float32)


def _mod_kernel(c_ref, w_ref, b_ref, o_ref):
    c = c_ref[...]
    s = (c * jax.nn.sigmoid(c)).astype(BF16)
    acc = jnp.dot(s, w_ref[...].astype(BF16), preferred_element_type=F32)
    o_ref[...] = acc + b_ref[...]


def _modulation(c_all, w_mod, b_mod):
    nb = c_all.shape[0]
    tn = 1024
    return pl.pallas_call(
        _mod_kernel,
        grid=(DEPTH, 3 * D_MODEL // tn),
        in_specs=[
            pl.BlockSpec((nb, D_MODEL), lambda l, j: (0, 0)),
            pl.BlockSpec((None, D_MODEL, tn), lambda l, j: (l, 0, j)),
            pl.BlockSpec((None, 1, tn), lambda l, j: (l, 0, j)),
        ],
        out_specs=pl.BlockSpec((None, nb, tn), lambda l, j: (l, 0, j)),
        out_shape=jax.ShapeDtypeStruct((DEPTH, nb, 3 * D_MODEL), F32),
        compiler_params=pltpu.CompilerParams(
            dimension_semantics=("parallel", "parallel"), vmem_limit_bytes=VMEM_LIMIT),
        name="modulation",
    )(c_all, w_mod, b_mod.reshape(DEPTH, 1, 3 * D_MODEL))


N_CHUNK = 1024


def _inproj_kernel(x_ref, mod_ref, w_ref, cs_ref, o_ref, *scratch, dil):
    tm = x_ref.shape[0]
    n = tm // dil
    ncol = w_ref.shape[1]
    shift = mod_ref[:, 0:D_MODEL]
    scale = mod_ref[:, D_MODEL:2 * D_MODEL]
    um = x_ref[...] * (1.0 + scale) + shift
    if dil == 1:
        u = um.astype(BF16)
    else:
        u_ref, xs_refs = scratch[0], scratch[1:]
        for c, xs_ref in enumerate(xs_refs):
            xs_ref[...] = um[:, c * LANES:(c + 1) * LANES]
        for r in range(dil):
            for c, xs_ref in enumerate(xs_refs):
                u_ref[r * n:(r + 1) * n, c * LANES:(c + 1) * LANES] = (
                    xs_ref[pl.ds(r, n, stride=dil), :].astype(BF16))
        u = u_ref[...]
    for c0 in range(0, ncol, N_CHUNK):
        c1 = min(c0 + N_CHUNK, ncol)
        acc = jnp.dot(u, w_ref[:, c0:c1], preferred_element_type=F32) * cs_ref[:, c0:c1]
        acc = acc.astype(BF16)
        for r in range(dil):
            o_ref[:, r * ncol + c0:r * ncol + c1] = acc[r * n:(r + 1) * n, :]


def _inproj(x, mod, w, colscale, dil):
    b, s, _ = x.shape
    ncol = w.shape[1]
    tm = 512
    n = tm // dil
    assert s % tm == 0 and n % 16 == 0
    scratch = []
    if dil > 1:
        scratch = [pltpu.VMEM((tm, D_MODEL), BF16)]
        scratch += [pltpu.VMEM((tm, LANES), F32) for _ in range(D_MODEL // LANES)]
    return pl.pallas_call(
        functools.partial(_inproj_kernel, dil=dil),
        grid=(b, s // tm),
        in_specs=[
            pl.BlockSpec((None, tm, D_MODEL), lambda bi, i: (bi, i, 0)),
            pl.BlockSpec((None, 1, 3 * D_MODEL), lambda bi, i: (bi, 0, 0)),
            pl.BlockSpec((D_MODEL, ncol), lambda bi, i: (0, 0)),
            pl.BlockSpec((1, ncol), lambda bi, i: (0, 0)),
        ],
        out_specs=pl.BlockSpec((None, n, dil * ncol), lambda bi, i: (bi, i, 0)),
        out_shape=jax.ShapeDtypeStruct((b, s // dil, dil * ncol), BF16),
        scratch_shapes=scratch,
        compiler_params=pltpu.CompilerParams(
            dimension_semantics=("parallel", "parallel"), vmem_limit_bytes=VMEM_LIMIT),
        name="inproj",
    )(x, mod, w, colscale)


def _attn_kernel(*refs, tq, halo, nk, kv_rep, has_sink, n_ub, sub_unroll):
    q_ref, ko_ref, kp_ref, kn_ref, vo_ref, vp_ref, vn_ref, tab_ref = refs[:8]
    pos = 8
    sink_ref = None
    if has_sink:
        sink_ref = refs[pos]
        pos += 1
    o_ref, m_ref, den_ref, kbuf, vbuf = refs[pos:pos + 5]

    ub = pl.program_id(2)
    n_sub = tq // MQ

    kbuf[0:halo] = kp_ref[...]
    kbuf[halo:halo + tq] = ko_ref[...]
    kbuf[halo + tq:halo + tq + halo] = kn_ref[...]
    vbuf[0:halo] = vp_ref[...]
    vbuf[halo:halo + tq] = vo_ref[...]
    vbuf[halo + tq:halo + tq + halo] = vn_ref[...]

    lane = lax.broadcasted_iota(jnp.int32, (MQ, LANES), 1)
    lo = lane < HEAD_DIM
    mask_lo = jnp.where(lo, 1.0, 0.0).astype(BF16)
    mask_hi = jnp.where(lo, 0.0, 1.0).astype(BF16)
    ones_blk = jnp.ones((nk, LANES), BF16)

    def block_rows(sb):
        r0 = sb * MQ if isinstance(sb, int) else pl.multiple_of(sb * MQ, MQ)
        return r0, pl.ds(r0, MQ)

    def score_stage(sb, pb):
        r0, rows = block_rows(sb)
        is_first = jnp.logical_and(ub == 0, sb == 0)
        is_last = jnp.logical_and(ub == n_ub - 1, sb == n_sub - 1)
        var = jnp.where(is_first, 0, jnp.where(is_last, 2, 1))
        kcols = slice((pb // kv_rep) * LANES, (pb // kv_rep + 1) * LANES)
        qp = q_ref[rows, pb * LANES:(pb + 1) * LANES]
        qs = jnp.concatenate([qp * mask_lo, qp * mask_hi], axis=0)
        kpair = kbuf[pl.ds(r0, nk), kcols]
        s = lax.dot_general(qs, kpair, (((1,), (1,)), ((), ())),
                            preferred_element_type=F32)
        return s + tab_ref[var, pb]

    def value_stage(sb, pb, s):
        r0, rows = block_rows(sb)
        kcols = slice((pb // kv_rep) * LANES, (pb // kv_rep + 1) * LANES)
        vpair = vbuf[pl.ds(r0, nk), kcols]
        m = jnp.max(s, axis=-1, keepdims=True)
        if has_sink:
            snk = [sink_ref[2 * pb + half] for half in range(2)]
            m = jnp.broadcast_to(m, (2 * MQ, LANES))
            m = jnp.concatenate([jnp.maximum(m[:MQ], snk[0]), jnp.maximum(m[MQ:], snk[1])],
                                axis=0)
            p = jnp.concatenate([jnp.exp2(s[:, t * LANES:(t + 1) * LANES] - m)
                                 for t in range(nk // LANES)], axis=1)
        else:
            p = jnp.exp2(s - m)
        if has_sink:
            den = jnp.broadcast_to(jnp.sum(p, axis=-1, keepdims=True), (2 * MQ, LANES))
            den = den + jnp.concatenate(
                [jnp.exp2(snk[0] - m[:MQ]), jnp.exp2(snk[1] - m[MQ:])], axis=0)
            pv = jnp.dot(p.astype(BF16), vpair, preferred_element_type=F32)
        else:
            pvd = jnp.dot(p.astype(BF16), jnp.concatenate([vpair, ones_blk], axis=1),
                          preferred_element_type=F32)
            pv, den = pvd[:, :LANES], pvd[:, LANES:]
        o_ref[rows, pb * LANES:(pb + 1) * LANES] = jnp.where(lo, pv[:MQ], pv[MQ:]).astype(BF16)
        for half in range(2):
            hl = slice(2 * pb + half, 2 * pb + half + 1)
            hl_src = hl if has_sink else slice(0, 1)
            m_ref[rows, hl] = m[half * MQ:(half + 1) * MQ, hl_src]
            den_ref[rows, hl] = den[half * MQ:(half + 1) * MQ, hl]

    def run_blocks(sbs):
        for sb in sbs:
            _, rows = block_rows(sb)
            m_ref[rows, :] = jnp.zeros((MQ, LANES), F32)
            den_ref[rows, :] = jnp.ones((MQ, LANES), F32)
        items = [(sb, pb) for sb in sbs for pb in range(N_PAIRS)]
        pending = [score_stage(*it) for it in items[:SCORE_LOOKAHEAD]]
        for i, it in enumerate(items):
            if i + SCORE_LOOKAHEAD < len(items):
                pending.append(score_stage(*items[i + SCORE_LOOKAHEAD]))
            value_stage(*it, pending.pop(0))

    if sub_unroll >= n_sub:
        run_blocks(list(range(n_sub)))
    else:
        def body(it, carry):
            run_blocks([it * sub_unroll + j for j in range(sub_unroll)])
            return carry
        lax.fori_loop(0, n_sub // sub_unroll, body, 0)


def _attention(view, *, dil, halo, q_col, k_col, v_col, kv_width, kv_rep, table, sink,
               sub_unroll):
    b, seq, dc = view.shape
    c = dc // dil
    tq = min(512, seq)
    n_ub = seq // tq
    nk = MQ + 2 * halo
    assert seq % tq == 0 and seq >= 2 * MQ and tq % halo == 0
    assert q_col % E_MIX == 0 and k_col % kv_width == 0 and v_col % kv_width == 0
    assert dil == 1 or (c % E_MIX == 0 and c % kv_width == 0)
    qblk, kblk, vblk = q_col // E_MIX, k_col // kv_width, v_col // kv_width
    hb = tq // halo
    last_hb = seq // halo - 1

    def own(width, cblk):
        per_tok = c // width
        return pl.BlockSpec((None, tq, width),
                            lambda bi, r, u: (bi, u, r * per_tok + cblk))

    def prev(width, cblk):
        per_tok = c // width
        return pl.BlockSpec((None, halo, width),
                            lambda bi, r, u: (bi, jnp.maximum(u * hb - 1, 0), r * per_tok + cblk))

    def nxt(width, cblk):
        per_tok = c // width
        return pl.BlockSpec((None, halo, width),
                            lambda bi, r, u: (bi, jnp.minimum((u + 1) * hb, last_hb), r * per_tok + cblk))

    in_specs = [
        own(E_MIX, qblk),
        own(kv_width, kblk), prev(kv_width, kblk), nxt(kv_width, kblk),
        own(kv_width, vblk), prev(kv_width, vblk), nxt(kv_width, vblk),
        pl.BlockSpec(table.shape, lambda bi, r, u: (0, 0, 0, 0)),
    ]
    args = [view] * 7 + [table]
    if sink is not None:
        in_specs.append(pl.BlockSpec(memory_space=pltpu.SMEM))
        args.append(sink)

    out_shape = [jax.ShapeDtypeStruct((b, seq, dil * E_MIX), BF16)]
    out_specs = [pl.BlockSpec((None, tq, E_MIX), lambda bi, r, u: (bi, u, r))]
    for _ in range(2):
        out_shape.append(jax.ShapeDtypeStruct((b, seq, dil * LANES), F32))
        out_specs.append(pl.BlockSpec((None, tq, LANES), lambda bi, r, u: (bi, u, r)))

    kern = functools.partial(
        _attn_kernel, tq=tq, halo=halo, nk=nk, kv_rep=kv_rep,
        has_sink=sink is not None, n_ub=n_ub, sub_unroll=sub_unroll)
    return pl.pallas_call(
        kern,
        grid=(b, dil, n_ub),
        in_specs=in_specs,
        out_specs=out_specs,
        out_shape=out_shape,
        scratch_shapes=[pltpu.VMEM((tq + 2 * halo, kv_width), BF16),
                        pltpu.VMEM((tq + 2 * halo, kv_width), BF16)],
        compiler_params=pltpu.CompilerParams(
            dimension_semantics=("parallel", "parallel", "parallel"),
            vmem_limit_bytes=VMEM_LIMIT),
        name="attention",
    )(*args)


PERM_ROWS = 256


def _out_kernel(*refs, dils):
    n_groups = len(dils)
    o_refs = refs[:n_groups]
    m_refs = refs[n_groups:2 * n_groups]
    d_refs = refs[2 * n_groups:3 * n_groups]
    pos = 3 * n_groups
    p_refs = {}
    for d in dils:
        if d > 1:
            p_refs[d] = refs[pos]
            pos += 1
    e_ref, z_ref, x_ref, mod_ref, w_ref, g_ref, b_ref, out_ref = refs[pos:pos + 8]
    stat_scratch = list(refs[pos + 8:])
    tm = x_ref.shape[0]

    def natural_rows(o_ref, d, t):
        if d == 1:
            return o_ref[t * PERM_ROWS:(t + 1) * PERM_ROWS, :].astype(F32)
        nb = PERM_ROWS // d
        stacked = jnp.concatenate(
            [o_ref[t * nb:(t + 1) * nb, r * E_MIX:(r + 1) * E_MIX] for r in range(d)], axis=0)
        return jnp.dot(p_refs[d][...], stacked, preferred_element_type=F32)

    def natural_stat(ref, d):
        if d == 1:
            return ref
        scr = stat_scratch.pop(0)
        n = tm // d
        for r in range(d):
            scr[pl.ds(r, n, stride=d), :] = ref[:, r * LANES:(r + 1) * LANES]
        return scr

    m_nat = [natural_stat(r, d) for r, d in zip(m_refs, dils)]
    d_nat = [natural_stat(r, d) for r, d in zip(d_refs, dils)]
    gate = mod_ref[:, 2 * D_MODEL:3 * D_MODEL]
    def merge_stage(t):
        rs = slice(t * PERM_ROWS, (t + 1) * PERM_ROWS)
        ms = [r[rs, :] for r in m_nat]
        dens = [r[rs, :] for r in d_nat]
        mx = ms[0]
        for m in ms[1:]:
            mx = jnp.maximum(mx, m)
        es = [jnp.exp2(m - mx) for m in ms]
        tot = es[0] * dens[0]
        for e, dn in zip(es[1:], dens[1:]):
            tot = tot + e * dn
        inv = 1.0 / tot
        att = None
        for e, o_ref, d in zip(es, o_refs, dils):
            wexp = jnp.dot((e * inv).astype(BF16), e_ref[...], preferred_element_type=F32)
            term = wexp * natural_rows(o_ref, d, t)
            att = term if att is None else att + term
        z = z_ref[rs, :].astype(F32)
        return (att * (z * jax.nn.sigmoid(z))).astype(BF16)

    def project_stage(t, h):
        rs = slice(t * PERM_ROWS, (t + 1) * PERM_ROWS)
        y = jnp.dot(h, w_ref[...], preferred_element_type=F32)
        v = DEEPNORM_ALPHA * x_ref[rs, :] + gate * y
        mu = jnp.mean(v, axis=-1, keepdims=True)
        vc = v - mu
        var = jnp.mean(vc * vc, axis=-1, keepdims=True)
        out_ref[rs, :] = vc * lax.rsqrt(var + LN_EPS) * g_ref[...] + b_ref[...]

    hs = [merge_stage(t) for t in range(tm // PERM_ROWS)]
    for t, h in enumerate(hs):
        project_stage(t, h)


def _perm_matrix(d):
    nb = PERM_ROWS // d
    p = np.zeros((PERM_ROWS, PERM_ROWS), np.float32)
    for r in range(d):
        for u in range(nb):
            p[u * d + r, r * nb + u] = 1.0
    return jnp.asarray(p, dtype=BF16)


def _outproj(attn_outs, dils, expand, zsrc, z_col, x, mod, w, g, bta):
    b, s, _ = x.shape
    tm = 1024
    assert z_col % E_MIX == 0 and s % tm == 0 and tm % PERM_ROWS == 0
    zblk = z_col // E_MIX
    row = lambda bi, i: (bi, i, 0)
    in_specs = [pl.BlockSpec((None, tm // d, d * E_MIX), row) for d in dils]
    in_specs += [pl.BlockSpec((None, tm // d, d * LANES), row) for d in dils] * 2
    args = [t[0] for t in attn_outs] + [t[1] for t in attn_outs] + [t[2] for t in attn_outs]
    scratch = []
    for d in dils:
        if d > 1:
            in_specs.append(pl.BlockSpec((PERM_ROWS, PERM_ROWS), lambda bi, i: (0, 0)))
            args.append(_perm_matrix(d))
            scratch += [pltpu.VMEM((tm, LANES), F32)] * 2
    in_specs.append(pl.BlockSpec(expand.shape, lambda bi, i: (0, 0)))
    args.append(expand)
    in_specs += [
        pl.BlockSpec((None, tm, E_MIX), lambda bi, i: (bi, i, zblk)),
        pl.BlockSpec((None, tm, D_MODEL), row),
        pl.BlockSpec((None, 1, 3 * D_MODEL), lambda bi, i: (bi, 0, 0)),
        pl.BlockSpec((E_MIX, D_MODEL), lambda bi, i: (0, 0)),
        pl.BlockSpec((1, D_MODEL), lambda bi, i: (0, 0)),
        pl.BlockSpec((1, D_MODEL), lambda bi, i: (0, 0)),
    ]
    args += [zsrc, x, mod, w, g, bta]
    return pl.pallas_call(
        functools.partial(_out_kernel, dils=tuple(dils)),
        grid=(b, s // tm),
        in_specs=in_specs,
        out_specs=pl.BlockSpec((None, tm, D_MODEL), row),
        out_shape=jax.ShapeDtypeStruct((b, s, D_MODEL), F32),
        scratch_shapes=scratch,
        compiler_params=pltpu.CompilerParams(
            dimension_semantics=("parallel", "parallel"), vmem_limit_bytes=VMEM_LIMIT),
        name="outproj",
    )(*args)


def _bias_table(head_of, dil, halo):
    nk = MQ + 2 * halo
    slopes = _alibi_slopes(N_HEADS)
    i = np.arange(MQ)[:, None]
    j = np.arange(nk)[None, :]
    rel = j - halo - i
    band = np.abs(rel) <= halo
    valid = np.stack([band & (j >= halo), band, band & (j < MQ + halo)])
    dist = (dil * np.abs(rel)).astype(np.float32)
    heads = np.asarray([[head_of(pb, 0), head_of(pb, 1)] for pb in range(N_PAIRS)])
    bias = -(slopes[heads][:, :, None, None] * dist[None, None]) * np.float32(LOG2E)
    tab = np.where(valid[:, None, None], bias[None], np.float32(NEG))
    return jnp.asarray(tab.reshape(3, N_PAIRS, 2 * MQ, nk), dtype=F32)


def _head_a(pb, half):
    return 2 * pb + half


def _head_b(pb, half):
    return (2 * (pb // REP_B) + half) * REP_B + pb % REP_B


def _colscale(n, q_cols):
    cs = np.ones((1, n), np.float32)
    for lo_, hi_ in q_cols:
        cs[:, lo_:hi_] = HEAD_DIM ** -0.5 * LOG2E
    return jnp.asarray(cs)


def _expand_matrix():
    e = np.zeros((LANES, E_MIX), np.float32)
    for h in range(N_HEADS):
        e[h, h * HEAD_DIM:(h + 1) * HEAD_DIM] = 1.0
    return jnp.asarray(e, dtype=BF16)


def _layer_a(x, mod, w_groups, w_out, cs_groups, tables, expand, g, bta):
    outs, dils = [], []
    zsrc = None
    for gi, (win, dil) in enumerate(A_GROUPS):
        view = _inproj(x, mod, w_groups[gi], cs_groups[gi], dil)
        if gi == 0:
            zsrc = view
        outs.append(_attention(view, dil=dil, halo=win // (2 * dil), q_col=0, k_col=E_MIX,
                               v_col=2 * E_MIX, kv_width=E_MIX, kv_rep=1, table=tables[gi],
                               sink=None, sub_unroll=4))
        dils.append(dil)
    return _outproj(outs, dils, expand, zsrc, 3 * E_MIX, x, mod, w_out, g, bta)


def _layer_b(x, mod, w_in, w_out, cs, table, sink, expand, g, bta):
    proj = _inproj(x, mod, w_in, cs, 1)
    out = _attention(proj, dil=1, halo=WIN_B, q_col=0, k_col=2 * E_MIX, v_col=2 * E_MIX + DKV_B,
                     kv_width=DKV_B, kv_rep=REP_B, table=table, sink=sink, sub_unroll=4)
    return _outproj([out], [1], expand, proj, E_MIX, x, mod, w_out, g, bta)


def kernel(x_prompt, x_sample, c_prompt, c_sample, w_mod, b_mod, ln_g, ln_b,
           w_in_a, w_out_a, w_in_b, w_out_b, sink_b):
    nbp = c_prompt.shape[0]
    mods = _modulation(jnp.concatenate([c_prompt, c_sample], axis=0), w_mod, b_mod)

    assert A_GROUPS[0][1] == 1
    grp = 3 * E_MIX
    w_a = w_in_a.astype(BF16)
    w_in_a16 = [[jnp.concatenate([w_a[l][:, :grp], w_a[l][:, QKV_A:]], axis=1)]
                + [w_a[l][:, gi * grp:(gi + 1) * grp] for gi in range(1, len(A_GROUPS))]
                for l in range(w_in_a.shape[0])]
    w_out_a16 = w_out_a.astype(BF16)

    def regroup_b(w, axis):
        shp = w.shape[:axis] + (KV_B // 2, 2, REP_B, HEAD_DIM) + w.shape[axis + 1:]
        w = jnp.swapaxes(w.reshape(shp), axis + 1, axis + 2)
        return w.reshape(w.shape[:axis] + (E_MIX,) + w.shape[axis + 4:])

    w_b = w_in_b.astype(BF16)
    w_in_b16 = jnp.concatenate([regroup_b(w_b[:, :, :E_MIX], 2),
                                regroup_b(w_b[:, :, E_MIX + 2 * DKV_B:], 2),
                                w_b[:, :, E_MIX:E_MIX + 2 * DKV_B]], axis=2)
    w_out_b16 = regroup_b(w_out_b.astype(BF16), 1)

    cs_a = [_colscale(w.shape[1], [(0, E_MIX)]) for w in w_in_a16[0]]
    cs_b = _colscale(IN_B, [(0, E_MIX)])
    tables_a = [_bias_table(_head_a, dil, win // (2 * dil)) for win, dil in A_GROUPS]
    table_b = _bias_table(_head_b, 1, WIN_B)
    expand = _expand_matrix()
    heads_b = np.asarray([[_head_b(pb, 0), _head_b(pb, 1)] for pb in range(N_PAIRS)])

    def trunk(x, mod_all):
        for l in range(DEPTH):
            mod = mod_all[l][:, None, :]
            g = ln_g[l][None, :]
            bta = ln_b[l][None, :]
            if l % 2 == 0:
                x = _layer_a(x, mod, w_in_a16[l // 2], w_out_a16[l // 2], cs_a, tables_a,
                             expand, g, bta)
            else:
                snk = (sink_b[l // 2][heads_b.reshape(-1)] * LOG2E).astype(F32)
                x = _layer_b(x, mod, w_in_b16[l // 2], w_out_b16[l // 2], cs_b, table_b,
                             snk, expand, g, bta)
        return x

    y_prompt = trunk(x_prompt, mods[:, :nbp])
    y_sample = trunk(x_sample, mods[:, nbp:])
    return (y_prompt, y_sample)
```

```python
import functools
import math

import numpy as np
import jax
import jax.numpy as jnp
from jax import lax
from jax.experimental import pallas as pl
from jax.experimental.pallas import tpu as pltpu

F32 = jnp.float32
BF16 = jnp.bfloat16

D_MODEL = 1024
DEPTH = 4
HEAD_DIM = 64
N_HEADS = 16
E_MIX = 1024
A_GROUPS = ((128, 1), (512, 4), (2048, 16))
QKV_A = 3 * len(A_GROUPS) * E_MIX
KV_B = 4
REP_B = N_HEADS // KV_B
WIN_B = 128
DKV_B = KV_B * HEAD_DIM
IN_B = E_MIX + 2 * DKV_B + E_MIX
DEEPNORM_ALPHA = (2.0 * DEPTH) ** 0.25
LN_EPS = 1e-5

LANES = 128
N_PAIRS = E_MIX // LANES
MQ = 128
SCORE_LOOKAHEAD = 2
LOG2E = math.log2(math.e)
NEG = -1e30
VMEM_LIMIT = 56 * 1024 * 1024


def _alibi_slopes(n):
    return np.asarray(2.0 ** (-8.0 * np.arange(1, n + 1) / n), dtype=np.float32)


def _mod_kernel(c_ref, w_ref, b_ref, o_ref):
    c = c_ref[...]
    s = (c * jax.nn.sigmoid(c)).astype(BF16)
    acc = jnp.dot(s, w_ref[...].astype(BF16), preferred_element_type=F32)
    o_ref[...] = acc + b_ref[...]


def _modulation(c_all, w_mod, b_mod):
    nb = c_all.shape[0]
    tn = 1024
    return pl.pallas_call(
        _mod_kernel,
        grid=(DEPTH, 3 * D_MODEL // tn),
        in_specs=[
            pl.BlockSpec((nb, D_MODEL), lambda l, j: (0, 0)),
            pl.BlockSpec((None, D_MODEL, tn), lambda l, j: (l, 0, j)),
            pl.BlockSpec((None, 1, tn), lambda l, j: (l, 0, j)),
        ],
        out_specs=pl.BlockSpec((None, nb, tn), lambda l, j: (l, 0, j)),
        out_shape=jax.ShapeDtypeStruct((DEPTH, nb, 3 * D_MODEL), F32),
        compiler_params=pltpu.CompilerParams(
            dimension_semantics=("parallel", "parallel"), vmem_limit_bytes=VMEM_LIMIT),
        name="modulation",
    )(c_all, w_mod, b_mod.reshape(DEPTH, 1, 3 * D_MODEL))


N_CHUNK = 1024
Q_SCALE = HEAD_DIM ** -0.5 * LOG2E


def _mod_spec(layer):
    return pl.BlockSpec((None, None, 1, 3 * D_MODEL), lambda bi, i: (layer, bi, 0, 0))


def _inproj_kernel(*refs, dil, n_w):
    x_ref, mod_ref = refs[:2]
    w_refs = refs[2:2 + n_w]
    o_ref = refs[2 + n_w]
    scratch = refs[3 + n_w:]
    tm = x_ref.shape[0]
    n = tm // dil
    ncol = sum(w_ref.shape[1] for w_ref in w_refs)
    shift = mod_ref[:, 0:D_MODEL]
    scale = mod_ref[:, D_MODEL:2 * D_MODEL]
    um = x_ref[...] * (1.0 + scale) + shift
    if dil == 1:
        u = um.astype(BF16)
    else:
        u_ref, xs_refs = scratch[0], scratch[1:]
        for c, xs_ref in enumerate(xs_refs):
            xs_ref[...] = um[:, c * LANES:(c + 1) * LANES]
        for r in range(dil):
            for c, xs_ref in enumerate(xs_refs):
                u_ref[r * n:(r + 1) * n, c * LANES:(c + 1) * LANES] = (
                    xs_ref[pl.ds(r, n, stride=dil), :].astype(BF16))
        u = u_ref[...]
    col = 0
    for w_ref in w_refs:
        width = w_ref.shape[1]
        for c0 in range(0, width, N_CHUNK):
            c1 = min(c0 + N_CHUNK, width)
            acc = jnp.dot(u, w_ref[:, c0:c1], preferred_element_type=F32)
            if col == 0:
                acc = acc * Q_SCALE
            acc = acc.astype(BF16)
            for r in range(dil):
                o_ref[:, r * ncol + col:r * ncol + col + c1 - c0] = acc[r * n:(r + 1) * n, :]
            col += c1 - c0


def _inproj(x, mods, layer, w, wl, pieces, dil):
    b, s, _ = x.shape
    ncol = sum(width for width, _ in pieces)
    tm = 512
    n = tm // dil
    assert s % tm == 0 and n % 16 == 0
    scratch = []
    if dil > 1:
        scratch = [pltpu.VMEM((tm, D_MODEL), BF16)]
        scratch += [pltpu.VMEM((tm, LANES), F32) for _ in range(D_MODEL // LANES)]
    in_specs = [pl.BlockSpec((None, tm, D_MODEL), lambda bi, i: (bi, i, 0)), _mod_spec(layer)]
    in_specs += [pl.BlockSpec((None, D_MODEL, width), lambda bi, i, c=cblk: (wl, 0, c))
                 for width, cblk in pieces]
    return pl.pallas_call(
        functools.partial(_inproj_kernel, dil=dil, n_w=len(pieces)),
        grid=(b, s // tm),
        in_specs=in_specs,
        out_specs=pl.BlockSpec((None, n, dil * ncol), lambda bi, i: (bi, i, 0)),
        out_shape=jax.ShapeDtypeStruct((b, s // dil, dil * ncol), BF16),
        scratch_shapes=scratch,
        compiler_params=pltpu.CompilerParams(
            dimension_semantics=("parallel", "parallel"), vmem_limit_bytes=VMEM_LIMIT),
        name="inproj",
    )(x, mods, *([w] * len(pieces)))


def _attn_kernel(*refs, tq, halo, nk, kv_rep, has_sink, n_ub, sub_unroll):
    q_ref, ko_ref, kp_ref, kn_ref, vo_ref, vp_ref, vn_ref, tab_ref = refs[:8]
    pos = 8
    sink_ref = None
    if has_sink:
        sink_ref = refs[pos]
        pos += 1
    o_ref, m_ref, den_ref, kbuf, vbuf = refs[pos:pos + 5]

    ub = pl.program_id(2)
    n_sub = tq // MQ

    kbuf[0:halo] = kp_ref[...]
    kbuf[halo:halo + tq] = ko_ref[...]
    kbuf[halo + tq:halo + tq + halo] = kn_ref[...]
    vbuf[0:halo] = vp_ref[...]
    vbuf[halo:halo + tq] = vo_ref[...]
    vbuf[halo + tq:halo + tq + halo] = vn_ref[...]

    lane = lax.broadcasted_iota(jnp.int32, (MQ, LANES), 1)
    lo = lane < HEAD_DIM
    mask_lo = jnp.where(lo, 1.0, 0.0).astype(BF16)
    mask_hi = jnp.where(lo, 0.0, 1.0).astype(BF16)
    ones_blk = jnp.ones((nk, LANES), BF16)

    def block_rows(sb):
        r0 = sb * MQ if isinstance(sb, int) else pl.multiple_of(sb * MQ, MQ)
        return r0, pl.ds(r0, MQ)

    def score_stage(sb, pb):
        r0, rows = block_rows(sb)
        is_first = jnp.logical_and(ub == 0, sb == 0)
        is_last = jnp.logical_and(ub == n_ub - 1, sb == n_sub - 1)
        var = jnp.where(is_first, 0, jnp.where(is_last, 2, 1))
        kcols = slice((pb // kv_rep) * LANES, (pb // kv_rep + 1) * LANES)
        qp = q_ref[rows, pb * LANES:(pb + 1) * LANES]
        qs = jnp.concatenate([qp * mask_lo, qp * mask_hi], axis=0)
        kpair = kbuf[pl.ds(r0, nk), kcols]
        s = lax.dot_general(qs, kpair, (((1,), (1,)), ((), ())),
                            preferred_element_type=F32)
        return s + tab_ref[var, pb]

    def value_stage(sb, pb, s):
        r0, rows = block_rows(sb)
        kcols = slice((pb // kv_rep) * LANES, (pb // kv_rep + 1) * LANES)
        vpair = vbuf[pl.ds(r0, nk), kcols]
        m = jnp.max(s, axis=-1, keepdims=True)
        if has_sink:
            snk = [sink_ref[2 * pb + half] for half in range(2)]
            m = jnp.broadcast_to(m, (2 * MQ, LANES))
            m = jnp.concatenate([jnp.maximum(m[:MQ], snk[0]), jnp.maximum(m[MQ:], snk[1])],
                                axis=0)
            p = jnp.concatenate([jnp.exp2(s[:, t * LANES:(t + 1) * LANES] - m)
                                 for t in range(nk // LANES)], axis=1)
        else:
            p = jnp.exp2(s - m)
        if has_sink:
            den = jnp.broadcast_to(jnp.sum(p, axis=-1, keepdims=True), (2 * MQ, LANES))
            den = den + jnp.concatenate(
                [jnp.exp2(snk[0] - m[:MQ]), jnp.exp2(snk[1] - m[MQ:])], axis=0)
            pv = jnp.dot(p.astype(BF16), vpair, preferred_element_type=F32)
        else:
            pvd = jnp.dot(p.astype(BF16), jnp.concatenate([vpair, ones_blk], axis=1),
                          preferred_element_type=F32)
            pv, den = pvd[:, :LANES], pvd[:, LANES:]
        o_ref[rows, pb * LANES:(pb + 1) * LANES] = jnp.where(lo, pv[:MQ], pv[MQ:]).astype(BF16)
        for half in range(2):
            hl = slice(2 * pb + half, 2 * pb + half + 1)
            hl_src = hl if has_sink else slice(0, 1)
            m_ref[rows, hl] = m[half * MQ:(half + 1) * MQ, hl_src]
            den_ref[rows, hl] = den[half * MQ:(half + 1) * MQ, hl]

    def run_blocks(sbs):
        for sb in sbs:
            _, rows = block_rows(sb)
            m_ref[rows, :] = jnp.zeros((MQ, LANES), F32)
            den_ref[rows, :] = jnp.ones((MQ, LANES), F32)
        items = [(sb, pb) for sb in sbs for pb in range(N_PAIRS)]
        pending = [score_stage(*it) for it in items[:SCORE_LOOKAHEAD]]
        for i, it in enumerate(items):
            if i + SCORE_LOOKAHEAD < len(items):
                pending.append(score_stage(*items[i + SCORE_LOOKAHEAD]))
            value_stage(*it, pending.pop(0))

    if sub_unroll >= n_sub:
        run_blocks(list(range(n_sub)))
    else:
        def body(it, carry):
            run_blocks([it * sub_unroll + j for j in range(sub_unroll)])
            return carry
        lax.fori_loop(0, n_sub // sub_unroll, body, 0)


def _attention(view, *, dil, halo, q_col, k_col, v_col, kv_width, kv_rep, table, sink,
               sub_unroll):
    b, seq, dc = view.shape
    c = dc // dil
    tq = min(512, seq)
    n_ub = seq // tq
    nk = MQ + 2 * halo
    assert seq % tq == 0 and seq >= 2 * MQ and tq % halo == 0
    assert q_col % E_MIX == 0 and k_col % kv_width == 0 and v_col % kv_width == 0
    assert dil == 1 or (c % E_MIX == 0 and c % kv_width == 0)
    qblk, kblk, vblk = q_col // E_MIX, k_col // kv_width, v_col // kv_width
    hb = tq // halo
    last_hb = seq // halo - 1

    def own(width, cblk):
        per_tok = c // width
        return pl.BlockSpec((None, tq, width),
                            lambda bi, r, u: (bi, u, r * per_tok + cblk))

    def prev(width, cblk):
        per_tok = c // width
        return pl.BlockSpec((None, halo, width),
                            lambda bi, r, u: (bi, jnp.maximum(u * hb - 1, 0), r * per_tok + cblk))

    def nxt(width, cblk):
        per_tok = c // width
        return pl.BlockSpec((None, halo, width),
                            lambda bi, r, u: (bi, jnp.minimum((u + 1) * hb, last_hb), r * per_tok + cblk))

    in_specs = [
        own(E_MIX, qblk),
        own(kv_width, kblk), prev(kv_width, kblk), nxt(kv_width, kblk),
        own(kv_width, vblk), prev(kv_width, vblk), nxt(kv_width, vblk),
        pl.BlockSpec(table.shape, lambda bi, r, u: (0, 0, 0, 0)),
    ]
    args = [view] * 7 + [table]
    if sink is not None:
        in_specs.append(pl.BlockSpec(memory_space=pltpu.SMEM))
        args.append(sink)

    out_shape = [jax.ShapeDtypeStruct((b, seq, dil * E_MIX), BF16)]
    out_specs = [pl.BlockSpec((None, tq, E_MIX), lambda bi, r, u: (bi, u, r))]
    for _ in range(2):
        out_shape.append(jax.ShapeDtypeStruct((b, seq, dil * LANES), F32))
        out_specs.append(pl.BlockSpec((None, tq, LANES), lambda bi, r, u: (bi, u, r)))

    kern = functools.partial(
        _attn_kernel, tq=tq, halo=halo, nk=nk, kv_rep=kv_rep,
        has_sink=sink is not None, n_ub=n_ub, sub_unroll=sub_unroll)
    return pl.pallas_call(
        kern,
        grid=(b, dil, n_ub),
        in_specs=in_specs,
        out_specs=out_specs,
        out_shape=out_shape,
        scratch_shapes=[pltpu.VMEM((tq + 2 * halo, kv_width), BF16),
                        pltpu.VMEM((tq + 2 * halo, kv_width), BF16)],
        compiler_params=pltpu.CompilerParams(
            dimension_semantics=("parallel", "parallel", "parallel"),
            vmem_limit_bytes=VMEM_LIMIT),
        name="attention",
    )(*args)


PERM_ROWS = 256


def _out_kernel(*refs, dils):
    n_groups = len(dils)
    o_refs = refs[:n_groups]
    m_refs = refs[n_groups:2 * n_groups]
    d_refs = refs[2 * n_groups:3 * n_groups]
    pos = 3 * n_groups
    p_refs = {}
    for d in dils:
        if d > 1:
            p_refs[d] = refs[pos]
            pos += 1
    e_ref, z_ref, x_ref, mod_ref, w_ref, g_ref, b_ref, out_ref = refs[pos:pos + 8]
    stat_scratch = list(refs[pos + 8:])
    tm = x_ref.shape[0]

    def natural_rows(o_ref, d, t):
        if d == 1:
            return o_ref[t * PERM_ROWS:(t + 1) * PERM_ROWS, :].astype(F32)
        nb = PERM_ROWS // d
        stacked = jnp.concatenate(
            [o_ref[t * nb:(t + 1) * nb, r * E_MIX:(r + 1) * E_MIX] for r in range(d)], axis=0)
        return jnp.dot(p_refs[d][...], stacked, preferred_element_type=F32)

    def natural_stat(ref, d):
        if d == 1:
            return ref
        scr = stat_scratch.pop(0)
        n = tm // d
        for r in range(d):
            scr[pl.ds(r, n, stride=d), :] = ref[:, r * LANES:(r + 1) * LANES]
        return scr

    m_nat = [natural_stat(r, d) for r, d in zip(m_refs, dils)]
    d_nat = [natural_stat(r, d) for r, d in zip(d_refs, dils)]
    gate = mod_ref[:, 2 * D_MODEL:3 * D_MODEL]
    def merge_stage(t):
        rs = slice(t * PERM_ROWS, (t + 1) * PERM_ROWS)
        ms = [r[rs, :] for r in m_nat]
        dens = [r[rs, :] for r in d_nat]
        mx = ms[0]
        for m in ms[1:]:
            mx = jnp.maximum(mx, m)
        es = [jnp.exp2(m - mx) for m in ms]
        tot = es[0] * dens[0]
        for e, dn in zip(es[1:], dens[1:]):
            tot = tot + e * dn
        inv = 1.0 / tot
        att = None
        for e, o_ref, d in zip(es, o_refs, dils):
            wexp = jnp.dot((e * inv).astype(BF16), e_ref[...], preferred_element_type=F32)
            term = wexp * natural_rows(o_ref, d, t)
            att = term if att is None else att + term
        z = z_ref[rs, :].astype(F32)
        return (att * (z * jax.nn.sigmoid(z))).astype(BF16)

    def project_stage(t, h):
        rs = slice(t * PERM_ROWS, (t + 1) * PERM_ROWS)
        y = jnp.dot(h, w_ref[...], preferred_element_type=F32)
        v = DEEPNORM_ALPHA * x_ref[rs, :] + gate * y
        mu = jnp.mean(v, axis=-1, keepdims=True)
        vc = v - mu
        var = jnp.mean(vc * vc, axis=-1, keepdims=True)
        out_ref[rs, :] = vc * lax.rsqrt(var + LN_EPS) * g_ref[...] + b_ref[...]

    hs = [merge_stage(t) for t in range(tm // PERM_ROWS)]
    for t, h in enumerate(hs):
        project_stage(t, h)


def _perm_matrix(d):
    nb = PERM_ROWS // d
    p = np.zeros((PERM_ROWS, PERM_ROWS), np.float32)
    for r in range(d):
        for u in range(nb):
            p[u * d + r, r * nb + u] = 1.0
    return jnp.asarray(p, dtype=BF16)


def _outproj(attn_outs, dils, expand, zsrc, z_col, x, mods, layer, w, wl, ln_g, ln_b):
    b, s, _ = x.shape
    tm = 1024
    assert z_col % E_MIX == 0 and s % tm == 0 and tm % PERM_ROWS == 0
    zblk = z_col // E_MIX
    row = lambda bi, i: (bi, i, 0)
    in_specs = [pl.BlockSpec((None, tm // d, d * E_MIX), row) for d in dils]
    in_specs += [pl.BlockSpec((None, tm // d, d * LANES), row) for d in dils] * 2
    args = [t[0] for t in attn_outs] + [t[1] for t in attn_outs] + [t[2] for t in attn_outs]
    scratch = []
    for d in dils:
        if d > 1:
            in_specs.append(pl.BlockSpec((PERM_ROWS, PERM_ROWS), lambda bi, i: (0, 0)))
            args.append(_perm_matrix(d))
            scratch += [pltpu.VMEM((tm, LANES), F32)] * 2
    in_specs.append(pl.BlockSpec(expand.shape, lambda bi, i: (0, 0)))
    args.append(expand)
    in_specs += [
        pl.BlockSpec((None, tm, E_MIX), lambda bi, i: (bi, i, zblk)),
        pl.BlockSpec((None, tm, D_MODEL), row),
        _mod_spec(layer),
        pl.BlockSpec((None, E_MIX, D_MODEL), lambda bi, i: (wl, 0, 0)),
        pl.BlockSpec((None, 1, D_MODEL), lambda bi, i: (layer, 0, 0)),
        pl.BlockSpec((None, 1, D_MODEL), lambda bi, i: (layer, 0, 0)),
    ]
    args += [zsrc, x, mods, w, ln_g, ln_b]
    return pl.pallas_call(
        functools.partial(_out_kernel, dils=tuple(dils)),
        grid=(b, s // tm),
        in_specs=in_specs,
        out_specs=pl.BlockSpec((None, tm, D_MODEL), row),
        out_shape=jax.ShapeDtypeStruct((b, s, D_MODEL), F32),
        scratch_shapes=scratch,
        compiler_params=pltpu.CompilerParams(
            dimension_semantics=("parallel", "parallel"), vmem_limit_bytes=VMEM_LIMIT),
        name="outproj",
    )(*args)


def _bias_table(head_of, dil, halo):
    nk = MQ + 2 * halo
    slopes = _alibi_slopes(N_HEADS)
    i = np.arange(MQ)[:, None]
    j = np.arange(nk)[None, :]
    rel = j - halo - i
    band = np.abs(rel) <= halo
    valid = np.stack([band & (j >= halo), band, band & (j < MQ + halo)])
    dist = (dil * np.abs(rel)).astype(np.float32)
    heads = np.asarray([[head_of(pb, 0), head_of(pb, 1)] for pb in range(N_PAIRS)])
    bias = -(slopes[heads][:, :, None, None] * dist[None, None]) * np.float32(LOG2E)
    tab = np.where(valid[:, None, None], bias[None], np.float32(NEG))
    return jnp.asarray(tab.reshape(3, N_PAIRS, 2 * MQ, nk), dtype=F32)


def _head_a(pb, half):
    return 2 * pb + half


def _head_b(pb, half):
    return (2 * (pb // REP_B) + half) * REP_B + pb % REP_B


def _expand_matrix():
    e = np.zeros((LANES, E_MIX), np.float32)
    for h in range(N_HEADS):
        e[h, h * HEAD_DIM:(h + 1) * HEAD_DIM] = 1.0
    return jnp.asarray(e, dtype=BF16)


def _layer_a(x, mods, layer, w_in, w_out, tables, expand, ln_g, ln_b):
    grp = 3 * E_MIX
    outs, dils = [], []
    zsrc = None
    for gi, (win, dil) in enumerate(A_GROUPS):
        pieces = [(grp, gi)] + ([(E_MIX, QKV_A // E_MIX)] if gi == 0 else [])
        view = _inproj(x, mods, layer, w_in, layer // 2, pieces, dil)
        if gi == 0:
            zsrc = view
        outs.append(_attention(view, dil=dil, halo=win // (2 * dil), q_col=0, k_col=E_MIX,
                               v_col=2 * E_MIX, kv_width=E_MIX, kv_rep=1, table=tables[gi],
                               sink=None, sub_unroll=4))
        dils.append(dil)
    return _outproj(outs, dils, expand, zsrc, grp, x, mods, layer, w_out, layer // 2, ln_g, ln_b)


def _layer_b(x, mods, layer, w_in, w_out, table, sink, expand, ln_g, ln_b):
    proj = _inproj(x, mods, layer, w_in, layer // 2, [(IN_B, 0)], 1)
    out = _attention(proj, dil=1, halo=WIN_B, q_col=0, k_col=2 * E_MIX, v_col=2 * E_MIX + DKV_B,
                     kv_width=DKV_B, kv_rep=REP_B, table=table, sink=sink, sub_unroll=4)
    return _outproj([out], [1], expand, proj, E_MIX, x, mods, layer, w_out, layer // 2, ln_g, ln_b)


def kernel(x_prompt, x_sample, c_prompt, c_sample, w_mod, b_mod, ln_g, ln_b,
           w_in_a, w_out_a, w_in_b, w_out_b, sink_b):
    nbp = c_prompt.shape[0]
    mods = _modulation(jnp.concatenate([c_prompt, c_sample], axis=0), w_mod, b_mod)
    mods = mods[:, :, None, :]

    assert A_GROUPS[0][1] == 1
    w_in_a16 = w_in_a.astype(BF16)
    w_out_a16 = w_out_a.astype(BF16)

    def regroup_b(w, axis):
        shp = w.shape[:axis] + (KV_B // 2, 2, REP_B, HEAD_DIM) + w.shape[axis + 1:]
        w = jnp.swapaxes(w.reshape(shp), axis + 1, axis + 2)
        return w.reshape(w.shape[:axis] + (E_MIX,) + w.shape[axis + 4:])

    w_b = w_in_b.astype(BF16)
    w_in_b16 = jnp.concatenate([regroup_b(w_b[:, :, :E_MIX], 2),
                                regroup_b(w_b[:, :, E_MIX + 2 * DKV_B:], 2),
                                w_b[:, :, E_MIX:E_MIX + 2 * DKV_B]], axis=2)
    w_out_b16 = regroup_b(w_out_b.astype(BF16), 1)

    tables_a = [_bias_table(_head_a, dil, win // (2 * dil)) for win, dil in A_GROUPS]
    table_b = _bias_table(_head_b, 1, WIN_B)
    expand = _expand_matrix()
    heads_b = np.asarray([[_head_b(pb, 0), _head_b(pb, 1)] for pb in range(N_PAIRS)])
    g3, b3 = ln_g[:, None, :], ln_b[:, None, :]

    def trunk(x, mods_x):
        for l in range(DEPTH):
            if l % 2 == 0:
                x = _layer_a(x, mods_x, l, w_in_a16, w_out_a16, tables_a, expand, g3, b3)
            else:
                snk = (sink_b[l // 2][heads_b.reshape(-1)] * LOG2E).astype(F32)
                x = _layer_b(x, mods_x, l, w_in_b16, w_out_b16, table_b, snk, expand, g3, b3)
        return x

    y_prompt = trunk(x_prompt, mods[:, :nbp])
    y_sample = trunk(x_sample, mods[:, nbp:])
    return (y_prompt, y_sample)
```

```python
import functools
import math

import numpy as np
import jax
import jax.numpy as jnp
from jax import lax
from jax.experimental import pallas as pl
from jax.experimental.pallas import tpu as pltpu

F32 = jnp.float32
BF16 = jnp.bfloat16

D_MODEL = 1024
DEPTH = 4
HEAD_DIM = 64
N_HEADS = 16
E_MIX = 1024
A_GROUPS = ((128, 1), (512, 4), (2048, 16))
QKV_A = 3 * len(A_GROUPS) * E_MIX
KV_B = 4
REP_B = N_HEADS // KV_B
WIN_B = 128
DKV_B = KV_B * HEAD_DIM
IN_B = E_MIX + 2 * DKV_B + E_MIX
DEEPNORM_ALPHA = (2.0 * DEPTH) ** 0.25
LN_EPS = 1e-5

LANES = 128
N_PAIRS = E_MIX // LANES
MQ = 128
SCORE_LOOKAHEAD = 2
LOG2E = math.log2(math.e)
NEG = -1e30
VMEM_LIMIT = 56 * 1024 * 1024


def _alibi_slopes(n):
    return np.asarray(2.0 ** (-8.0 * np.arange(1, n + 1) / n), dtype=np.float32)


def _mod_kernel(c_ref, w_ref, b_ref, o_ref):
    c = c_ref[...]
    s = (c * jax.nn.sigmoid(c)).astype(BF16)
    acc = jnp.dot(s, w_ref[...].astype(BF16), preferred_element_type=F32)
    o_ref[...] = acc + b_ref[...]


def _modulation(c_all, w_mod, b_mod):
    nb = c_all.shape[0]
    tn = 1024
    return pl.pallas_call(
        _mod_kernel,
        grid=(DEPTH, 3 * D_MODEL // tn),
        in_specs=[
            pl.BlockSpec((nb, D_MODEL), lambda l, j: (0, 0)),
            pl.BlockSpec((None, D_MODEL, tn), lambda l, j: (l, 0, j)),
            pl.BlockSpec((None, 1, tn), lambda l, j: (l, 0, j)),
        ],
        out_specs=pl.BlockSpec((None, nb, tn), lambda l, j: (l, 0, j)),
        out_shape=jax.ShapeDtypeStruct((DEPTH, nb, 3 * D_MODEL), F32),
        compiler_params=pltpu.CompilerParams(
            dimension_semantics=("parallel", "parallel"), vmem_limit_bytes=VMEM_LIMIT),
        name="modulation",
    )(c_all, w_mod, b_mod.reshape(DEPTH, 1, 3 * D_MODEL))


N_CHUNK = 1024
Q_SCALE = HEAD_DIM ** -0.5 * LOG2E


def _mod_spec(layer):
    return pl.BlockSpec((None, None, 1, 3 * D_MODEL), lambda bi, i: (layer, bi, 0, 0))


def _inproj_kernel(*refs, dil, n_w):
    x_ref, mod_ref = refs[:2]
    w_refs = refs[2:2 + n_w]
    o_ref = refs[2 + n_w]
    scratch = refs[3 + n_w:]
    tm = x_ref.shape[0]
    n = tm // dil
    ncol = sum(w_ref.shape[1] for w_ref in w_refs)
    shift = mod_ref[:, 0:D_MODEL]
    scale = mod_ref[:, D_MODEL:2 * D_MODEL]
    um = x_ref[...] * (1.0 + scale) + shift
    if dil == 1:
        u = um.astype(BF16)
    else:
        u_ref, xs_refs = scratch[0], scratch[1:]
        for c, xs_ref in enumerate(xs_refs):
            xs_ref[...] = um[:, c * LANES:(c + 1) * LANES]
        for r in range(dil):
            for c, xs_ref in enumerate(xs_refs):
                u_ref[r * n:(r + 1) * n, c * LANES:(c + 1) * LANES] = (
                    xs_ref[pl.ds(r, n, stride=dil), :].astype(BF16))
        u = u_ref[...]
    col = 0
    for w_ref in w_refs:
        width = w_ref.shape[1]
        for c0 in range(0, width, N_CHUNK):
            c1 = min(c0 + N_CHUNK, width)
            acc = jnp.dot(u, w_ref[:, c0:c1], preferred_element_type=F32)
            if col == 0:
                acc = acc * Q_SCALE
            acc = acc.astype(BF16)
            for r in range(dil):
                o_ref[:, r * ncol + col:r * ncol + col + c1 - c0] = acc[r * n:(r + 1) * n, :]
            col += c1 - c0


def _inproj(x, mods, layer, w, wl, pieces, dil):
    b, s, _ = x.shape
    ncol = sum(width for width, _ in pieces)
    tm = 512 * min(2, 16 // dil)
    n = tm // dil
    assert s % tm == 0 and n % 16 == 0
    scratch = []
    if dil > 1:
        scratch = [pltpu.VMEM((tm, D_MODEL), BF16)]
        scratch += [pltpu.VMEM((tm, LANES), F32) for _ in range(D_MODEL // LANES)]
    in_specs = [pl.BlockSpec((None, tm, D_MODEL), lambda bi, i: (bi, i, 0)), _mod_spec(layer)]
    in_specs += [pl.BlockSpec((None, D_MODEL, width), lambda bi, i, c=cblk: (wl, 0, c))
                 for width, cblk in pieces]
    return pl.pallas_call(
        functools.partial(_inproj_kernel, dil=dil, n_w=len(pieces)),
        grid=(b, s // tm),
        in_specs=in_specs,
        out_specs=pl.BlockSpec((None, n, dil * ncol), lambda bi, i: (bi, i, 0)),
        out_shape=jax.ShapeDtypeStruct((b, s // dil, dil * ncol), BF16),
        scratch_shapes=scratch,
        compiler_params=pltpu.CompilerParams(
            dimension_semantics=("parallel", "parallel"), vmem_limit_bytes=VMEM_LIMIT),
        name="inproj",
    )(x, mods, *([w] * len(pieces)))


def _attn_kernel(*refs, tq, halo, nk, kv_rep, has_sink, n_ub, sub_unroll):
    q_ref, ko_ref, kp_ref, kn_ref, vo_ref, vp_ref, vn_ref, tab_ref = refs[:8]
    pos = 8
    sink_ref = None
    if has_sink:
        sink_ref = refs[pos]
        pos += 1
    o_ref, m_ref, den_ref, kbuf, vbuf = refs[pos:pos + 5]

    ub = pl.program_id(2)
    n_sub = tq // MQ

    kbuf[0:halo] = kp_ref[...]
    kbuf[halo:halo + tq] = ko_ref[...]
    kbuf[halo + tq:halo + tq + halo] = kn_ref[...]
    vbuf[0:halo] = vp_ref[...]
    vbuf[halo:halo + tq] = vo_ref[...]
    vbuf[halo + tq:halo + tq + halo] = vn_ref[...]

    lane = lax.broadcasted_iota(jnp.int32, (MQ, LANES), 1)
    lo = lane < HEAD_DIM
    mask_lo = jnp.where(lo, 1.0, 0.0).astype(BF16)
    mask_hi = jnp.where(lo, 0.0, 1.0).astype(BF16)
    ones_blk = jnp.ones((nk, LANES), BF16)

    def block_rows(sb):
        r0 = sb * MQ if isinstance(sb, int) else pl.multiple_of(sb * MQ, MQ)
        return r0, pl.ds(r0, MQ)

    def score_stage(sb, pb):
        r0, rows = block_rows(sb)
        is_first = jnp.logical_and(ub == 0, sb == 0)
        is_last = jnp.logical_and(ub == n_ub - 1, sb == n_sub - 1)
        var = jnp.where(is_first, 0, jnp.where(is_last, 2, 1))
        kcols = slice((pb // kv_rep) * LANES, (pb // kv_rep + 1) * LANES)
        qp = q_ref[rows, pb * LANES:(pb + 1) * LANES]
        qs = jnp.concatenate([qp * mask_lo, qp * mask_hi], axis=0)
        kpair = kbuf[pl.ds(r0, nk), kcols]
        s = lax.dot_general(qs, kpair, (((1,), (1,)), ((), ())),
                            preferred_element_type=F32)
        return s + tab_ref[var, pb]

    def value_stage(sb, pb, s):
        r0, rows = block_rows(sb)
        kcols = slice((pb // kv_rep) * LANES, (pb // kv_rep + 1) * LANES)
        vpair = vbuf[pl.ds(r0, nk), kcols]
        m = jnp.max(s, axis=-1, keepdims=True)
        if has_sink:
            snk = [sink_ref[2 * pb + half] for half in range(2)]
            m = jnp.broadcast_to(m, (2 * MQ, LANES))
            m = jnp.concatenate([jnp.maximum(m[:MQ], snk[0]), jnp.maximum(m[MQ:], snk[1])],
                                axis=0)
            p = jnp.concatenate([jnp.exp2(s[:, t * LANES:(t + 1) * LANES] - m)
                                 for t in range(nk // LANES)], axis=1)
        else:
            p = jnp.exp2(s - m)
        if has_sink:
            den = jnp.broadcast_to(jnp.sum(p, axis=-1, keepdims=True), (2 * MQ, LANES))
            den = den + jnp.concatenate(
                [jnp.exp2(snk[0] - m[:MQ]), jnp.exp2(snk[1] - m[MQ:])], axis=0)
            pv = jnp.dot(p.astype(BF16), vpair, preferred_element_type=F32)
        else:
            pvd = jnp.dot(p.astype(BF16), jnp.concatenate([vpair, ones_blk], axis=1),
                          preferred_element_type=F32)
            pv, den = pvd[:, :LANES], pvd[:, LANES:]
        o_ref[rows, pb * LANES:(pb + 1) * LANES] = jnp.where(lo, pv[:MQ], pv[MQ:]).astype(BF16)
        for half in range(2):
            hl = slice(2 * pb + half, 2 * pb + half + 1)
            hl_src = hl if has_sink else slice(0, 1)
            m_ref[rows, hl] = m[half * MQ:(half + 1) * MQ, hl_src]
            den_ref[rows, hl] = den[half * MQ:(half + 1) * MQ, hl]

    def run_blocks(sbs):
        for sb in sbs:
            _, rows = block_rows(sb)
            m_ref[rows, :] = jnp.zeros((MQ, LANES), F32)
            den_ref[rows, :] = jnp.ones((MQ, LANES), F32)
        items = [(sb, pb) for sb in sbs for pb in range(N_PAIRS)]
        pending = [score_stage(*it) for it in items[:SCORE_LOOKAHEAD]]
        for i, it in enumerate(items):
            if i + SCORE_LOOKAHEAD < len(items):
                pending.append(score_stage(*items[i + SCORE_LOOKAHEAD]))
            value_stage(*it, pending.pop(0))

    if sub_unroll >= n_sub:
        run_blocks(list(range(n_sub)))
    else:
        def body(it, carry):
            run_blocks([it * sub_unroll + j for j in range(sub_unroll)])
            return carry
        lax.fori_loop(0, n_sub // sub_unroll, body, 0)


def _attention(view, *, dil, halo, q_col, k_col, v_col, kv_width, kv_rep, table, sink,
               sub_unroll):
    b, seq, dc = view.shape
    c = dc // dil
    tq = min(512, seq)
    n_ub = seq // tq
    nk = MQ + 2 * halo
    assert seq % tq == 0 and seq >= 2 * MQ and tq % halo == 0
    assert q_col % E_MIX == 0 and k_col % kv_width == 0 and v_col % kv_width == 0
    assert dil == 1 or (c % E_MIX == 0 and c % kv_width == 0)
    qblk, kblk, vblk = q_col // E_MIX, k_col // kv_width, v_col // kv_width
    hb = tq // halo
    last_hb = seq // halo - 1

    def own(width, cblk):
        per_tok = c // width
        return pl.BlockSpec((None, tq, width),
                            lambda bi, r, u: (bi, u, r * per_tok + cblk))

    def prev(width, cblk):
        per_tok = c // width
        return pl.BlockSpec((None, halo, width),
                            lambda bi, r, u: (bi, jnp.maximum(u * hb - 1, 0), r * per_tok + cblk))

    def nxt(width, cblk):
        per_tok = c // width
        return pl.BlockSpec((None, halo, width),
                            lambda bi, r, u: (bi, jnp.minimum((u + 1) * hb, last_hb), r * per_tok + cblk))

    in_specs = [
        own(E_MIX, qblk),
        own(kv_width, kblk), prev(kv_width, kblk), nxt(kv_width, kblk),
        own(kv_width, vblk), prev(kv_width, vblk), nxt(kv_width, vblk),
        pl.BlockSpec(table.shape, lambda bi, r, u: (0, 0, 0, 0)),
    ]
    args = [view] * 7 + [table]
    if sink is not None:
        in_specs.append(pl.BlockSpec(memory_space=pltpu.SMEM))
        args.append(sink)

    out_shape = [jax.ShapeDtypeStruct((b, seq, dil * E_MIX), BF16)]
    out_specs = [pl.BlockSpec((None, tq, E_MIX), lambda bi, r, u: (bi, u, r))]
    for _ in range(2):
        out_shape.append(jax.ShapeDtypeStruct((b, seq, dil * LANES), F32))
        out_specs.append(pl.BlockSpec((None, tq, LANES), lambda bi, r, u: (bi, u, r)))

    kern = functools.partial(
        _attn_kernel, tq=tq, halo=halo, nk=nk, kv_rep=kv_rep,
        has_sink=sink is not None, n_ub=n_ub, sub_unroll=sub_unroll)
    return pl.pallas_call(
        kern,
        grid=(b, dil, n_ub),
        in_specs=in_specs,
        out_specs=out_specs,
        out_shape=out_shape,
        scratch_shapes=[pltpu.VMEM((tq + 2 * halo, kv_width), BF16),
                        pltpu.VMEM((tq + 2 * halo, kv_width), BF16)],
        compiler_params=pltpu.CompilerParams(
            dimension_semantics=("parallel", "parallel", "parallel"),
            vmem_limit_bytes=VMEM_LIMIT),
        name="attention",
    )(*args)


PERM_ROWS = 256


def _out_kernel(*refs, dils):
    n_groups = len(dils)
    o_refs = refs[:n_groups]
    m_refs = refs[n_groups:2 * n_groups]
    d_refs = refs[2 * n_groups:3 * n_groups]
    pos = 3 * n_groups
    p_refs = {}
    for d in dils:
        if d > 1:
            p_refs[d] = refs[pos]
            pos += 1
    e_ref, z_ref, x_ref, mod_ref, w_ref, g_ref, b_ref, out_ref = refs[pos:pos + 8]
    stat_scratch = list(refs[pos + 8:])
    tm = x_ref.shape[0]

    def natural_rows(o_ref, d, t):
        if d == 1:
            return o_ref[t * PERM_ROWS:(t + 1) * PERM_ROWS, :].astype(F32)
        nb = PERM_ROWS // d
        stacked = jnp.concatenate(
            [o_ref[t * nb:(t + 1) * nb, r * E_MIX:(r + 1) * E_MIX] for r in range(d)], axis=0)
        return jnp.dot(p_refs[d][...], stacked, preferred_element_type=F32)

    def natural_stat(ref, d):
        if d == 1:
            return ref
        scr = stat_scratch.pop(0)
        n = tm // d
        for r in range(d):
            scr[pl.ds(r, n, stride=d), :] = ref[:, r * LANES:(r + 1) * LANES]
        return scr

    m_nat = [natural_stat(r, d) for r, d in zip(m_refs, dils)]
    d_nat = [natural_stat(r, d) for r, d in zip(d_refs, dils)]
    gate = mod_ref[:, 2 * D_MODEL:3 * D_MODEL]
    def merge_stage(t):
        rs = slice(t * PERM_ROWS, (t + 1) * PERM_ROWS)
        ms = [r[rs, :] for r in m_nat]
        dens = [r[rs, :] for r in d_nat]
        mx = ms[0]
        for m in ms[1:]:
            mx = jnp.maximum(mx, m)
        es = [jnp.exp2(m - mx) for m in ms]
        tot = es[0] * dens[0]
        for e, dn in zip(es[1:], dens[1:]):
            tot = tot + e * dn
        inv = 1.0 / tot
        att = None
        for e, o_ref, d in zip(es, o_refs, dils):
            wexp = jnp.dot((e * inv).astype(BF16), e_ref[...], preferred_element_type=F32)
            term = wexp * natural_rows(o_ref, d, t)
            att = term if att is None else att + term
        z = z_ref[rs, :].astype(F32)
        return (att * (z * jax.nn.sigmoid(z))).astype(BF16)

    def project_stage(t, h):
        rs = slice(t * PERM_ROWS, (t + 1) * PERM_ROWS)
        y = jnp.dot(h, w_ref[...], preferred_element_type=F32)
        v = DEEPNORM_ALPHA * x_ref[rs, :] + gate * y
        mu = jnp.mean(v, axis=-1, keepdims=True)
        vc = v - mu
        var = jnp.mean(vc * vc, axis=-1, keepdims=True)
        out_ref[rs, :] = vc * lax.rsqrt(var + LN_EPS) * g_ref[...] + b_ref[...]

    hs = [merge_stage(t) for t in range(tm // PERM_ROWS)]
    for t, h in enumerate(hs):
        project_stage(t, h)


def _perm_matrix(d):
    nb = PERM_ROWS // d
    p = np.zeros((PERM_ROWS, PERM_ROWS), np.float32)
    for r in range(d):
        for u in range(nb):
            p[u * d + r, r * nb + u] = 1.0
    return jnp.asarray(p, dtype=BF16)


def _outproj(attn_outs, dils, expand, zsrc, z_col, x, mods, layer, w, wl, ln_g, ln_b):
    b, s, _ = x.shape
    tm = 1024
    assert z_col % E_MIX == 0 and s % tm == 0 and tm % PERM_ROWS == 0
    zblk = z_col // E_MIX
    row = lambda bi, i: (bi, i, 0)
    in_specs = [pl.BlockSpec((None, tm // d, d * E_MIX), row) for d in dils]
    in_specs += [pl.BlockSpec((None, tm // d, d * LANES), row) for d in dils] * 2
    args = [t[0] for t in attn_outs] + [t[1] for t in attn_outs] + [t[2] for t in attn_outs]
    scratch = []
    for d in dils:
        if d > 1:
            in_specs.append(pl.BlockSpec((PERM_ROWS, PERM_ROWS), lambda bi, i: (0, 0)))
            args.append(_perm_matrix(d))
            scratch += [pltpu.VMEM((tm, LANES), F32)] * 2
    in_specs.append(pl.BlockSpec(expand.shape, lambda bi, i: (0, 0)))
    args.append(expand)
    in_specs += [
        pl.BlockSpec((None, tm, E_MIX), lambda bi, i: (bi, i, zblk)),
        pl.BlockSpec((None, tm, D_MODEL), row),
        _mod_spec(layer),
        pl.BlockSpec((None, E_MIX, D_MODEL), lambda bi, i: (wl, 0, 0)),
        pl.BlockSpec((None, 1, D_MODEL), lambda bi, i: (layer, 0, 0)),
        pl.BlockSpec((None, 1, D_MODEL), lambda bi, i: (layer, 0, 0)),
    ]
    args += [zsrc, x, mods, w, ln_g, ln_b]
    return pl.pallas_call(
        functools.partial(_out_kernel, dils=tuple(dils)),
        grid=(b, s // tm),
        in_specs=in_specs,
        out_specs=pl.BlockSpec((None, tm, D_MODEL), row),
        out_shape=jax.ShapeDtypeStruct((b, s, D_MODEL), F32),
        scratch_shapes=scratch,
        compiler_params=pltpu.CompilerParams(
            dimension_semantics=("parallel", "parallel"), vmem_limit_bytes=VMEM_LIMIT),
        name="outproj",
    )(*args)


def _bias_table(head_of, dil, halo):
    nk = MQ + 2 * halo
    slopes = _alibi_slopes(N_HEADS)
    i = np.arange(MQ)[:, None]
    j = np.arange(nk)[None, :]
    rel = j - halo - i
    band = np.abs(rel) <= halo
    valid = np.stack([band & (j >= halo), band, band & (j < MQ + halo)])
    dist = (dil * np.abs(rel)).astype(np.float32)
    heads = np.asarray([[head_of(pb, 0), head_of(pb, 1)] for pb in range(N_PAIRS)])
    bias = -(slopes[heads][:, :, None, None] * dist[None, None]) * np.float32(LOG2E)
    tab = np.where(valid[:, None, None], bias[None], np.float32(NEG))
    return jnp.asarray(tab.reshape(3, N_PAIRS, 2 * MQ, nk), dtype=F32)


def _head_a(pb, half):
    return 2 * pb + half


def _head_b(pb, half):
    return (2 * (pb // REP_B) + half) * REP_B + pb % REP_B


def _expand_matrix():
    e = np.zeros((LANES, E_MIX), np.float32)
    for h in range(N_HEADS):
        e[h, h * HEAD_DIM:(h + 1) * HEAD_DIM] = 1.0
    return jnp.asarray(e, dtype=BF16)


def _layer_a(x, mods, layer, w_in, w_out, tables, expand, ln_g, ln_b):
    grp = 3 * E_MIX
    outs, dils = [], []
    zsrc = None
    for gi, (win, dil) in enumerate(A_GROUPS):
        pieces = [(grp, gi)] + ([(E_MIX, QKV_A // E_MIX)] if gi == 0 else [])
        view = _inproj(x, mods, layer, w_in, layer // 2, pieces, dil)
        if gi == 0:
            zsrc = view
        outs.append(_attention(view, dil=dil, halo=win // (2 * dil), q_col=0, k_col=E_MIX,
                               v_col=2 * E_MIX, kv_width=E_MIX, kv_rep=1, table=tables[gi],
                               sink=None, sub_unroll=4))
        dils.append(dil)
    return _outproj(outs, dils, expand, zsrc, grp, x, mods, layer, w_out, layer // 2, ln_g, ln_b)


def _layer_b(x, mods, layer, w_in, w_out, table, sink, expand, ln_g, ln_b):
    proj = _inproj(x, mods, layer, w_in, layer // 2, [(IN_B, 0)], 1)
    out = _attention(proj, dil=1, halo=WIN_B, q_col=0, k_col=2 * E_MIX, v_col=2 * E_MIX + DKV_B,
                     kv_width=DKV_B, kv_rep=REP_B, table=table, sink=sink, sub_unroll=4)
    return _outproj([out], [1], expand, proj, E_MIX, x, mods, layer, w_out, layer // 2, ln_g, ln_b)


def kernel(x_prompt, x_sample, c_prompt, c_sample, w_mod, b_mod, ln_g, ln_b,
           w_in_a, w_out_a, w_in_b, w_out_b, sink_b):
    nbp = c_prompt.shape[0]
    mods = _modulation(jnp.concatenate([c_prompt, c_sample], axis=0), w_mod, b_mod)
    mods = mods[:, :, None, :]

    assert A_GROUPS[0][1] == 1
    w_in_a16 = w_in_a.astype(BF16)
    w_out_a16 = w_out_a.astype(BF16)

    def regroup_b(w, axis):
        shp = w.shape[:axis] + (KV_B // 2, 2, REP_B, HEAD_DIM) + w.shape[axis + 1:]
        w = jnp.swapaxes(w.reshape(shp), axis + 1, axis + 2)
        return w.reshape(w.shape[:axis] + (E_MIX,) + w.shape[axis + 4:])

    w_b = w_in_b.astype(BF16)
    w_in_b16 = jnp.concatenate([regroup_b(w_b[:, :, :E_MIX], 2),
                                regroup_b(w_b[:, :, E_MIX + 2 * DKV_B:], 2),
                                w_b[:, :, E_MIX:E_MIX + 2 * DKV_B]], axis=2)
    w_out_b16 = regroup_b(w_out_b.astype(BF16), 1)

    tables_a = [_bias_table(_head_a, dil, win // (2 * dil)) for win, dil in A_GROUPS]
    table_b = _bias_table(_head_b, 1, WIN_B)
    expand = _expand_matrix()
    heads_b = np.asarray([[_head_b(pb, 0), _head_b(pb, 1)] for pb in range(N_PAIRS)])
    g3, b3 = ln_g[:, None, :], ln_b[:, None, :]

    def trunk(x, mods_x):
        for l in range(DEPTH):
            if l % 2 == 0:
                x = _layer_a(x, mods_x, l, w_in_a16, w_out_a16, tables_a, expand, g3, b3)
            else:
                snk = (sink_b[l // 2][heads_b.reshape(-1)] * LOG2E).astype(F32)
                x = _layer_b(x, mods_x, l, w_in_b16, w_out_b16, table_b, snk, expand, g3, b3)
        return x

    y_prompt = trunk(x_prompt, mods[:, :nbp])
    y_sample = trunk(x_sample, mods[:, nbp:])
    return (y_prompt, y_sample)
```
